```python
import math
import jax, jax.numpy as jnp
from jax import lax
import numpy as np


D_MODEL = 1024
BATCH = 16
SEQ = 2048
DEPTH = 2

GRID_W = 64
CTX_LEN = 256
N_GROUPS = 4
GROUP_W = D_MODEL // N_GROUPS
HEADS = 4
HEAD_DIM = GROUP_W // HEADS
CHUNK = 128
DIFF_QK = HEAD_DIM // 2
ROT_HALF = DIFF_QK // 2
ROPE_THETA = 10000.0
Q_BLOCK = 128
D_FF = 2816
EPS = 1e-6
G_AU, G_AV, G_BX, G_BB, G_BC, G_CF, G_DQ, G_DK, G_DV = range(9)
IN_COLS = 9 * GROUP_W
KV_COL0 = G_DK * GROUP_W

kernel_name = 'hybrid_parallel_head_dit_block'


def rmsnorm(x, g=None):
    xf = x.astype(jnp.float32)
    y = xf * lax.rsqrt(jnp.mean(xf * xf, axis=-1, keepdims=True) + EPS)
    if g is not None:
        y = y * g.astype(jnp.float32)
    return y.astype(x.dtype)


def adaln_params(cvec, w_ada, b_ada):
    m = jax.nn.silu(cvec) @ w_ada + b_ada
    return jnp.split(m, 6, axis=-1)


def modulate(h, shift, scale):
    return h * (1 + scale) + shift


def group(z, i):
    return z[..., i * GROUP_W:(i + 1) * GROUP_W]


def dwconv3(x, w):
    xp = jnp.pad(x, ((0, 0), (1, 1), (0, 0)))
    return xp[:, :-2] * w[0] + xp[:, 1:-1] * w[1] + xp[:, 2:] * w[2]


def mixer_a(z, ws, bs):
    B, N, _ = z.shape
    u = jax.nn.gelu(group(z, G_AU))
    v = rmsnorm(jax.nn.gelu(group(z, G_AV)).reshape(B, N, HEADS, HEAD_DIM))
    v = v.reshape(B, N // CHUNK, CHUNK, HEADS, HEAD_DIM)
    mixed = jnp.einsum('hpq,bcqhd->bcphd', ws, v) + bs.T[:, :, None]
    return u * mixed.reshape(B, N, GROUP_W)


def mixer_b(z, w):
    return group(z, G_BB) * dwconv3(group(z, G_BC) * group(z, G_BX), w)


def mixer_c(z):
    B, N, _ = z.shape
    f = group(z, G_CF).reshape(B, N, HEADS, HEAD_DIM).astype(jnp.float32)
    F = jnp.fft.fft2(f, axes=(1, 3), norm='ortho').real
    return F.astype(z.dtype).reshape(B, N, GROUP_W)


def apply_rope_2d(t, ang_r, ang_c):
    def rot(u, ang):
        cos = jnp.cos(ang)[None, :, None, None, :].astype(u.dtype)
        sin = jnp.sin(ang)[None, :, None, None, :].astype(u.dtype)
        u1, u2 = u[..., :ROT_HALF // 2], u[..., ROT_HALF // 2:]
        return jnp.concatenate([u1 * cos - u2 * sin, u2 * cos + u1 * sin], axis=-1)
    return jnp.concatenate([rot(t[..., :ROT_HALF], ang_r), rot(t[..., ROT_HALF:], ang_c)], axis=-1)


def diff_attend(q, k, v, lam):
    s = jnp.einsum('bqhmd,bkhmd->bhmqk', q, k).astype(jnp.float32) * (DIFF_QK ** -0.5)
    p = jax.nn.softmax(s, axis=-1)
    a = p[:, :, 0] - lam * p[:, :, 1]
    return jnp.einsum('bhqk,bkhd->bqhd', a.astype(v.dtype), v)


def diff_out(o, g, lam_init):
    B, N = o.shape[0], o.shape[1]
    return (rmsnorm(o, g) * (1 - lam_init)).reshape(B, N, GROUP_W)


def conv_ffn(h, w_up, conv_w, w_down):
    up = h @ w_up
    a, b = up[..., :D_FF], up[..., D_FF:]
    return (jax.nn.silu(dwconv3(a, conv_w)) * b) @ w_down


def trunk_layer(x, xc, c, c_ctx, ang_r, ang_c, layer_idx, ctx_out,
                w_ada, b_ada, g1, g2, w_in, gm_ws, gm_bs, sc_w,
                lq1, lk1, lq2, lk2, subln_g, w_out, w_up, ffn_conv, w_down):
    B, N, _ = x.shape
    L = xc.shape[1]
    sh1, sc1, gt1, sh2, sc2, gt2 = [m[:, None, :] for m in adaln_params(c, w_ada, b_ada)]
    shc1, scc1, gtc1, shc2, scc2, gtc2 = adaln_params(c_ctx, w_ada, b_ada)
    lam_init = 0.8 - 0.6 * math.exp(-0.3 * layer_idx)
    f32 = jnp.float32
    lam = (jnp.exp(jnp.sum(lq1.astype(f32) * lk1.astype(f32)))
           - jnp.exp(jnp.sum(lq2.astype(f32) * lk2.astype(f32))) + lam_init)

    h = modulate(rmsnorm(x, g1), sh1, sc1)
    hc = modulate(rmsnorm(xc, g1), shc1, scc1)
    z = h @ w_in
    zc_kv = hc @ w_in[:, KV_COL0:]
    kc = zc_kv[..., :GROUP_W].reshape(B, L, HEADS, 2, DIFF_QK)
    vc = zc_kv[..., GROUP_W:].reshape(B, L, HEADS, HEAD_DIM)

    q = apply_rope_2d(group(z, G_DQ).reshape(B, N, HEADS, 2, DIFF_QK), ang_r, ang_c)
    k = apply_rope_2d(group(z, G_DK).reshape(B, N, HEADS, 2, DIFF_QK), ang_r, ang_c)
    v = group(z, G_DV).reshape(B, N, HEADS, HEAD_DIM)
    k_all = jnp.concatenate([kc, k], axis=1)
    v_all = jnp.concatenate([vc, v], axis=1)
    qb = q.reshape(B, N // Q_BLOCK, Q_BLOCK, HEADS, 2, DIFF_QK).swapaxes(0, 1)
    o = lax.map(lambda qi: diff_attend(qi, k_all, v_all, lam), qb)
    o = o.swapaxes(0, 1).reshape(B, N, HEADS, HEAD_DIM)

    mix = jnp.concatenate([mixer_a(z, gm_ws, gm_bs), mixer_b(z, sc_w), mixer_c(z),
                           diff_out(o, subln_g, lam_init)], axis=-1)
    x = x + gt1 * (mix @ w_out)
    x = x + gt2 * conv_ffn(modulate(rmsnorm(x, g2), sh2, sc2), w_up, ffn_conv, w_down)
    if not ctx_out:
        return x, None

    zc = hc @ w_in[:, :KV_COL0]
    qc = group(zc, G_DQ).reshape(B, L, HEADS, 2, DIFF_QK)
    oc = diff_attend(qc, kc, vc, lam)
    mixc = jnp.concatenate([mixer_a(zc, gm_ws, gm_bs), mixer_b(zc, sc_w), mixer_c(zc),
                            diff_out(oc, subln_g, lam_init)], axis=-1)
    xc = xc + gtc1 * (mixc @ w_out)
    xc = xc + gtc2 * conv_ffn(modulate(rmsnorm(xc, g2), shc2, scc2), w_up, ffn_conv, w_down)
    return x, xc


def setup_inputs(seed: int = 0) -> dict:
    key = jax.random.key(seed)
    ks = jax.random.split(key, 24)
    D = D_MODEL

    def nrm(k, shape, scale):
        return jax.random.normal(k, shape, jnp.float32) * scale

    return {
        'x': nrm(ks[0], (BATCH, SEQ, D), 1.0),
        'c': nrm(ks[1], (BATCH, D), 1.0),
        'ctx': nrm(ks[2], (BATCH, CTX_LEN, D), 1.0),
        'c_ctx': nrm(ks[3], (D,), 1.0),
        'w_ada': nrm(ks[4], (DEPTH, D, 6 * D), 0.5 * D ** -0.5),
        'b_ada': nrm(ks[5], (DEPTH, 6 * D), 0.02),
        'norm1_g': 1.0 + nrm(ks[6], (DEPTH, D), 0.02),
        'norm2_g': 1.0 + nrm(ks[7], (DEPTH, D), 0.02),
        'w_in': nrm(ks[8], (DEPTH, D, IN_COLS), D ** -0.5),
        'gmlp_ws': nrm(ks[9], (DEPTH, HEADS, CHUNK, CHUNK), CHUNK ** -0.5),
        'gmlp_bs': 1.0 + nrm(ks[10], (DEPTH, HEADS, CHUNK), 0.02),
        'sconv_w': nrm(ks[11], (DEPTH, 3, GROUP_W), 3 ** -0.5),
        'lambda_q1': nrm(ks[12], (DEPTH, DIFF_QK), 0.1),
        'lambda_k1': nrm(ks[13], (DEPTH, DIFF_QK), 0.1),
        'lambda_q2': nrm(ks[14], (DEPTH, DIFF_QK), 0.1),
        'lambda_k2': nrm(ks[15], (DEPTH, DIFF_QK), 0.1),
        'subln_g': 1.0 + nrm(ks[16], (DEPTH, HEAD_DIM), 0.02),
        'w_out': nrm(ks[17], (DEPTH, D, D), D ** -0.5),
        'ffn_w_up': nrm(ks[18], (DEPTH, D, 2 * D_FF), D ** -0.5),
        'ffn_conv_w': nrm(ks[19], (DEPTH, 3, D_FF), 3 ** -0.5),
        'ffn_w_down': nrm(ks[20], (DEPTH, D_FF, D), D_FF ** -0.5),
        'final_g': 1.0 + nrm(ks[21], (D,), 0.02),
    }


def reference(x, c, ctx, c_ctx, w_ada, b_ada, norm1_g, norm2_g, w_in, gmlp_ws, gmlp_bs,
              sconv_w, lambda_q1, lambda_k1, lambda_q2, lambda_k2, subln_g, w_out,
              ffn_w_up, ffn_conv_w, ffn_w_down, final_g):
    N = x.shape[1]
    ROWS = N // GRID_W
    row = jnp.broadcast_to(jnp.arange(ROWS, dtype=jnp.float32)[:, None], (ROWS, GRID_W)).reshape(-1)
    col = jnp.broadcast_to(jnp.arange(GRID_W, dtype=jnp.float32)[None, :], (ROWS, GRID_W)).reshape(-1)
    inv_freq = ROPE_THETA ** (-jnp.arange(0, ROT_HALF, 2, dtype=jnp.float32) / ROT_HALF)
    ang_r = row[:, None] * inv_freq
    ang_c = col[:, None] * inv_freq
    xc = ctx
    for l in range(DEPTH):
        x, xc = trunk_layer(
            x, xc, c, c_ctx, ang_r, ang_c, l, l < DEPTH - 1,
            w_ada[l], b_ada[l], norm1_g[l], norm2_g[l], w_in[l], gmlp_ws[l], gmlp_bs[l],
            sconv_w[l], lambda_q1[l], lambda_k1[l], lambda_q2[l], lambda_k2[l], subln_g[l],
            w_out[l], ffn_w_up[l], ffn_conv_w[l], ffn_w_down[l])
    return rmsnorm(x, final_g)
```

```python
import functools
import math

import numpy as np
import jax
import jax.numpy as jnp
from jax import lax
from jax.experimental import pallas as pl
from jax.experimental.pallas import tpu as pltpu

D_MODEL = 1024
DEPTH = 2
GRID_W = 64
N_GROUPS = 4
GROUP_W = D_MODEL // N_GROUPS
HEADS = 4
HEAD_DIM = GROUP_W // HEADS
CHUNK = 128
DIFF_QK = HEAD_DIM // 2
ROT_HALF = DIFF_QK // 2
ROPE_THETA = 10000.0
D_FF = 2816
EPS = 1e-6
G_AU, G_AV, G_BX, G_BB, G_BC, G_CF, G_DQ, G_DK, G_DV = range(9)
IN_COLS = 9 * GROUP_W

F32 = jnp.float32
BF16 = jnp.bfloat16

ADA_ROWS = 24
CTX_ROW = 16
FFN_HALO = 16
FFN_CHUNKS = 2
VMEM_LIMIT = 56 * 1024 * 1024


def _cparams(sem):
    return pltpu.CompilerParams(dimension_semantics=sem, vmem_limit_bytes=VMEM_LIMIT)


def _const_spec(shape):
    nd = len(shape)
    return pl.BlockSpec(shape, lambda *_: (0,) * nd, pipeline_mode=pl.Buffered(1))


def _rms(x):
    return x * lax.rsqrt(jnp.mean(x * x, axis=-1, keepdims=True) + EPS)


def _group_mean(sq, bd):
    hi = sq.astype(BF16)
    lo = (sq - hi.astype(F32)).astype(BF16)
    return (jnp.dot(hi, bd, preferred_element_type=F32)
            + jnp.dot(lo, bd, preferred_element_type=F32))


def _ada_body(c_ref, w_ref, b_ref, o_ref):
    s = jax.nn.silu(c_ref[...]).astype(BF16)
    o_ref[0] = jnp.dot(s, w_ref[0].astype(BF16), preferred_element_type=F32) + b_ref[0]


def _adaln(cc, w_ada, b_ada):
    n_col = 6
    return pl.pallas_call(
        _ada_body,
        grid=(DEPTH, n_col),
        in_specs=[
            pl.BlockSpec((ADA_ROWS, D_MODEL), lambda l, j: (0, 0)),
            pl.BlockSpec((1, D_MODEL, D_MODEL), lambda l, j: (l, 0, j)),
            pl.BlockSpec((1, 1, D_MODEL), lambda l, j: (l, 0, j)),
        ],
        out_specs=pl.BlockSpec((1, ADA_ROWS, D_MODEL), lambda l, j: (l, 0, j)),
        out_shape=jax.ShapeDtypeStruct((DEPTH, ADA_ROWS, 6 * D_MODEL), F32),
        compiler_params=_cparams(("parallel", "parallel")),
        name="adaln",
    )(cc, w_ada, b_ada.reshape(DEPTH, 1, 6 * D_MODEL))


def _in_body(x_ref, g_ref, sh_ref, sc_ref, w_ref, *rest, rope):
    if rope:
        cos_ref, sa_ref, sb_ref, z_ref = rest
    else:
        (z_ref,) = rest
    x = x_ref[0]
    h = _rms(x) * g_ref[...]
    h = h * (1.0 + sc_ref[0]) + sh_ref[0]
    z = jnp.dot(h.astype(BF16), w_ref[...], preferred_element_type=F32)
    q0 = G_DQ * GROUP_W
    z_ref[0, :, :q0] = z[:, :q0].astype(BF16)
    for g in (G_DQ, G_DK):
        t = z[:, g * GROUP_W:(g + 1) * GROUP_W]
        if rope:
            t = (t * cos_ref[...]
                 + pltpu.roll(t, GROUP_W - ROT_HALF // 2, axis=1) * sa_ref[...]
                 + pltpu.roll(t, ROT_HALF // 2, axis=1) * sb_ref[...])
        z_ref[0, :, g * GROUP_W:(g + 1) * GROUP_W] = t.astype(BF16)
    z_ref[0, :, G_DV * GROUP_W:] = z[:, G_DV * GROUP_W:].astype(BF16)


def _in_proj(x, g1, mod, w_in_bf, rope_tabs, *, ctx, ts):
    B, S, D = x.shape
    brow = (lambda b: CTX_ROW) if ctx else (lambda b: b)
    in_specs = [
        pl.BlockSpec((1, ts, D), lambda i, b: (b, i, 0)),
        _const_spec((1, D)),
        pl.BlockSpec((1, 1, D), lambda i, b: (brow(b), 0, 0)),
        pl.BlockSpec((1, 1, D), lambda i, b: (brow(b), 0, 1)),
        _const_spec((D, IN_COLS)),
    ]
    args = [x, g1.reshape(1, D), mod, mod, w_in_bf]
    rope = rope_tabs is not None
    if rope:
        in_specs += [pl.BlockSpec((ts, GROUP_W), lambda i, b: (i, 0))] * 3
        args += list(rope_tabs)
    return pl.pallas_call(
        functools.partial(_in_body, rope=rope),
        grid=(S // ts, B),
        in_specs=in_specs,
        out_specs=pl.BlockSpec((1, ts, IN_COLS), lambda i, b: (b, i, 0)),
        out_shape=jax.ShapeDtypeStruct((B, S, IN_COLS), BF16),
        compiler_params=_cparams(("parallel", "parallel")),
        name="in_proj_ctx" if ctx else "in_proj",
    )(*args)


def _gelu(x):
    return jax.nn.gelu(x)


def _mix_body(z_ref, ws_ref, bias_ref, sc_ref, bd_ref, c64_ref, s64_ref, wc_ref, wsn_ref,
              o_ref, *, seq):
    lane = lax.broadcasted_iota(jnp.int32, (1, GROUP_W), 1)
    head_of_lane = lax.shift_right_logical(lane, 6)
    bd = bd_ref[...]
    ws = ws_ref[...]
    bias = bias_ref[...]

    def chunk(c, carry):
        r = pl.multiple_of(c * CHUNK, CHUNK)
        au = z_ref[0, pl.ds(r, CHUNK), G_AU * GROUP_W:(G_AU + 1) * GROUP_W].astype(F32)
        av = z_ref[0, pl.ds(r, CHUNK), G_AV * GROUP_W:(G_AV + 1) * GROUP_W].astype(F32)
        u = _gelu(au)
        g = _gelu(av)
        v = (g * lax.rsqrt(_group_mean(g * g, bd) + EPS)).astype(BF16)
        vst = jnp.concatenate(
            [jnp.where(head_of_lane == h, v, jnp.zeros_like(v)) for h in range(HEADS)], axis=0)
        mixed = jnp.dot(ws, vst, preferred_element_type=F32) + bias
        o_ref[0, pl.ds(r, CHUNK), 0:GROUP_W] = (u * mixed).astype(BF16)
        return carry

    lax.fori_loop(0, seq // CHUNK, chunk, 0)

    bx = z_ref[0, :, G_BX * GROUP_W:(G_BX + 1) * GROUP_W].astype(F32)
    bb = z_ref[0, :, G_BB * GROUP_W:(G_BB + 1) * GROUP_W].astype(F32)
    bc = z_ref[0, :, G_BC * GROUP_W:(G_BC + 1) * GROUP_W].astype(F32)
    cx = bc * bx
    row = lax.broadcasted_iota(jnp.int32, (seq, 1), 0)
    prev = jnp.where(row == 0, 0.0, pltpu.roll(cx, 1, axis=0))
    nxt = jnp.where(row == seq - 1, 0.0, pltpu.roll(cx, seq - 1, axis=0))
    w = sc_ref[...]
    conv = prev * w[0:1] + cx * w[1:2] + nxt * w[2:3]
    o_ref[0, :, GROUP_W:2 * GROUP_W] = (bb * conv).astype(BF16)

    f = z_ref[0, :, G_CF * GROUP_W:(G_CF + 1) * GROUP_W]
    gc = jnp.dot(f, c64_ref[...], preferred_element_type=F32).astype(BF16)
    gs = jnp.dot(f, s64_ref[...], preferred_element_type=F32).astype(BF16)
    fo = (jnp.dot(wc_ref[...], gc, preferred_element_type=F32)
          + jnp.dot(wsn_ref[...], gs, preferred_element_type=F32))
    o_ref[0, :, 2 * GROUP_W:3 * GROUP_W] = fo.astype(BF16)


def _mixers(z, ws_stack, bias_full, sconv, bd, c64, s64, wc, wsn, *, ctx):
    B, S, _ = z.shape
    zc = (G_CF + 1) * GROUP_W
    return pl.pallas_call(
        functools.partial(_mix_body, seq=S),
        grid=(B,),
        in_specs=[
            pl.BlockSpec((1, S, zc), lambda b: (b, 0, 0)),
            _const_spec((CHUNK, HEADS * CHUNK)),
            _const_spec((CHUNK, GROUP_W)),
            _const_spec((3, GROUP_W)),
            _const_spec((GROUP_W, GROUP_W)),
            _const_spec((GROUP_W, GROUP_W)),
            _const_spec((GROUP_W, GROUP_W)),
            _const_spec((S, S)),
            _const_spec((S, S)),
        ],
        out_specs=pl.BlockSpec((1, S, 3 * GROUP_W), lambda b: (b, 0, 0)),
        out_shape=jax.ShapeDtypeStruct((B, S, 3 * GROUP_W), BF16),
        compiler_params=_cparams(("parallel",)),
        name="mixers_ctx" if ctx else "mixers",
    )(z, ws_stack, bias_full, sconv, bd, c64, s64, wc, wsn)


def _attn_body(q_ref, *rest, n_src, lam_init):
    k_refs = rest[0:2 * n_src:2]
    v_refs = rest[1:2 * n_src:2]
    lam_ref, g_ref, bd_ref, o_ref = rest[2 * n_src:]
    q = q_ref[0]
    if n_src > 1:
        kall = jnp.concatenate([r[0] for r in k_refs], axis=0)
        vall = jnp.concatenate([r[0] for r in v_refs], axis=0)
    else:
        kall = k_refs[0][0]
        vall = v_refs[0][0]
    lp = lam_ref[...]
    lam = (jnp.exp(jnp.sum(lp[0:1] * lp[1:2], axis=-1, keepdims=True))
           - jnp.exp(jnp.sum(lp[2:3] * lp[3:4], axis=-1, keepdims=True)) + lam_init)
    lane = lax.broadcasted_iota(jnp.int32, (1, GROUP_W), 1)
    map_of_lane = lax.shift_right_logical(lane, 5)
    head_of_lane = lax.shift_right_logical(lane, 6)
    scale = DIFF_QK ** -0.5
    o = jnp.zeros((q.shape[0], GROUP_W), F32)
    for h in range(HEADS):
        es, ls = [], []
        for m in range(2):
            qm = jnp.where(map_of_lane == 2 * h + m, q, jnp.zeros_like(q))
            s = lax.dot_general(qm, kall, (((1,), (1,)), ((), ())),
                                preferred_element_type=F32) * scale
            e = jnp.exp(s - jnp.max(s, axis=-1, keepdims=True))
            es.append(e)
            ls.append(jnp.sum(e, axis=-1, keepdims=True))
        a = es[0] * (1.0 / ls[0]) - es[1] * (lam / ls[1])
        oh = jnp.dot(a.astype(BF16), vall, preferred_element_type=F32)
        o = jnp.where(head_of_lane == h, oh, o)
    y = o * lax.rsqrt(_group_mean(o * o, bd_ref[...]) + EPS) * g_ref[...]
    o_ref[0] = (y * (1.0 - lam_init)).astype(BF16)


def _attention(zq, kv_srcs, lam_params, subln_row, bd, *, lam_init, tq, ctx):
    B, S, _ = zq.shape
    in_specs = [pl.BlockSpec((1, tq, GROUP_W), lambda b, i: (b, i, G_DQ))]
    args = [zq]
    for src in kv_srcs:
        L = src.shape[1]
        in_specs.append(pl.BlockSpec((1, L, GROUP_W), lambda b, i: (b, 0, G_DK)))
        in_specs.append(pl.BlockSpec((1, L, GROUP_W), lambda b, i: (b, 0, G_DV)))
        args += [src, src]
    in_specs += [_const_spec((4, DIFF_QK)), _const_spec((1, GROUP_W)),
                 _const_spec((GROUP_W, GROUP_W))]
    args += [lam_params, subln_row, bd]
    return pl.pallas_call(
        functools.partial(_attn_body, n_src=len(kv_srcs), lam_init=lam_init),
        grid=(B, S // tq),
        in_specs=in_specs,
        out_specs=pl.BlockSpec((1, tq, GROUP_W), lambda b, i: (b, i, 0)),
        out_shape=jax.ShapeDtypeStruct((B, S, GROUP_W), BF16),
        compiler_params=_cparams(("parallel", "parallel")),
        name="attn_ctx" if ctx else "attn",
    )(*args)


def _out_body(x_ref, mabc_ref, md_ref, w_ref, gt_ref, o_ref):
    k_abc = 3 * GROUP_W
    y = (jnp.dot(mabc_ref[0], w_ref[:k_abc, :], preferred_element_type=F32)
         + jnp.dot(md_ref[0], w_ref[k_abc:, :], preferred_element_type=F32))
    o_ref[0] = x_ref[0] + gt_ref[0] * y


def _out_proj(x, mabc, md, w_out_bf, mod, *, ctx, ts):
    B, S, D = x.shape
    brow = (lambda b: CTX_ROW) if ctx else (lambda b: b)
    return pl.pallas_call(
        _out_body,
        grid=(S // ts, B),
        in_specs=[
            pl.BlockSpec((1, ts, D), lambda i, b: (b, i, 0)),
            pl.BlockSpec((1, ts, 3 * GROUP_W), lambda i, b: (b, i, 0)),
            pl.BlockSpec((1, ts, GROUP_W), lambda i, b: (b, i, 0)),
            _const_spec((D, D)),
            pl.BlockSpec((1, 1, D), lambda i, b: (brow(b), 0, 2)),
        ],
        out_specs=pl.BlockSpec((1, ts, D), lambda i, b: (b, i, 0)),
        out_shape=jax.ShapeDtypeStruct((B, S, D), F32),
        compiler_params=_cparams(("parallel", "parallel")),
        name="out_proj_ctx" if ctx else "out_proj",
    )(x, mabc, md, w_out_bf, mod)


def _ffn_body(x_ref, xp_ref, xn_ref, g_ref, sh_ref, sc_ref, gt_ref, wa_ref, wb_ref, cw_ref,
              wd_ref, *rest, ts, final):
    if final:
        fg_ref, o_ref, h_s, a_s = rest
    else:
        o_ref, h_s, a_s = rest
    i = pl.program_id(1)
    last = pl.num_programs(1) - 1
    g = g_ref[...]
    sh = sh_ref[0]
    sc = sc_ref[0]

    def norm2(t):
        return (_rms(t) * g) * (1.0 + sc) + sh

    x = x_ref[0]
    H = FFN_HALO
    h_s[0:H, :] = jnp.where(i > 0, norm2(xp_ref[0]), 0.0).astype(BF16)
    h_s[H:H + ts, :] = norm2(x).astype(BF16)
    h_s[H + ts:, :] = jnp.where(i < last, norm2(xn_ref[0]), 0.0).astype(BF16)

    acc = jnp.zeros((ts, D_MODEL), F32)
    for j in range(FFN_CHUNKS):
        a_s[...] = jnp.dot(h_s[...], wa_ref[j], preferred_element_type=F32)
        cw = cw_ref[j]
        conv = (a_s[pl.ds(H - 1, ts), :] * cw[0:1]
                + a_s[pl.ds(H, ts), :] * cw[1:2]
                + a_s[pl.ds(H + 1, ts), :] * cw[2:3])
        b = jnp.dot(h_s[H:H + ts, :], wb_ref[j], preferred_element_type=F32)
        hid = (jax.nn.silu(conv) * b).astype(BF16)
        acc = acc + jnp.dot(hid, wd_ref[j], preferred_element_type=F32)
    y = x + gt_ref[0] * acc
    if final:
        y = _rms(y) * fg_ref[...]
    o_ref[0] = y


def _ffn(x, g2, mod, wa3, wb3, cw3, wd3, final_g, *, ctx, ts):
    B, S, D = x.shape
    brow = (lambda b: CTX_ROW) if ctx else (lambda b: b)
    fc = D_FF // FFN_CHUNKS
    H = FFN_HALO
    nh = S // H
    per = ts // H
    final = final_g is not None
    in_specs = [
        pl.BlockSpec((1, ts, D), lambda b, i: (b, i, 0)),
        pl.BlockSpec((1, H, D), lambda b, i: (b, jnp.maximum(i * per - 1, 0), 0)),
        pl.BlockSpec((1, H, D), lambda b, i: (b, jnp.minimum((i + 1) * per, nh - 1), 0)),
        _const_spec((1, D)),
        pl.BlockSpec((1, 1, D), lambda b, i: (brow(b), 0, 3)),
        pl.BlockSpec((1, 1, D), lambda b, i: (brow(b), 0, 4)),
        pl.BlockSpec((1, 1, D), lambda b, i: (brow(b), 0, 5)),
        _const_spec((FFN_CHUNKS, D, fc)),
        _const_spec((FFN_CHUNKS, D, fc)),
        _const_spec((FFN_CHUNKS, 3, fc)),
        _const_spec((FFN_CHUNKS, fc, D)),
    ]
    args = [x, x, x, g2.reshape(1, D), mod, mod, mod, wa3, wb3, cw3, wd3]
    if final:
        in_specs.append(_const_spec((1, D)))
        args.append(final_g.reshape(1, D))
    return pl.pallas_call(
        functools.partial(_ffn_body, ts=ts, final=final),
        grid=(B, S // ts),
        in_specs=in_specs,
        out_specs=pl.BlockSpec((1, ts, D), lambda b, i: (b, i, 0)),
        out_shape=jax.ShapeDtypeStruct((B, S, D), F32),
        scratch_shapes=[pltpu.VMEM((ts + 2 * H, D), BF16),
                        pltpu.VMEM((ts + 2 * H, fc), F32)],
        compiler_params=_cparams(("parallel", "parallel")),
        name="ffn_ctx" if ctx else "ffn",
    )(*args)


def _rope_tables(n):
    pos = jnp.arange(n, dtype=jnp.int32)
    row = (pos // GRID_W).astype(F32)
    col = (pos % GRID_W).astype(F32)
    inv_freq = ROPE_THETA ** (-jnp.arange(0, ROT_HALF, 2, dtype=F32) / ROT_HALF)
    ang_r = row[:, None] * inv_freq
    ang_c = col[:, None] * inv_freq
    lane = np.arange(GROUP_W)
    axis = (lane // ROT_HALF) % 2
    half = (lane // (ROT_HALF // 2)) % 2
    j = lane % (ROT_HALF // 2)
    ang = jnp.where(jnp.asarray(axis == 0)[None, :], ang_r[:, j], ang_c[:, j])
    cos = jnp.cos(ang)
    sin = jnp.sin(ang)
    sa = jnp.where(jnp.asarray(half == 0)[None, :], -sin, 0.0)
    sb = jnp.where(jnp.asarray(half == 1)[None, :], sin, 0.0)
    return cos, sa, sb


def _dft_tables(n):
    k = jnp.arange(n, dtype=jnp.int32)
    kn = (k[:, None] * k[None, :]) % n
    ang = kn.astype(F32) * (2.0 * math.pi / n)
    s = 1.0 / math.sqrt(n)
    return (jnp.cos(ang) * s).astype(BF16), (jnp.sin(ang) * (-s)).astype(BF16)


def _head_block_tables():
    eye = np.kron(np.eye(HEADS), np.ones((HEAD_DIM, HEAD_DIM)))
    k = np.arange(HEAD_DIM)
    ang = 2.0 * np.pi * ((k[:, None] * k[None, :]) % HEAD_DIM) / HEAD_DIM
    s = 1.0 / math.sqrt(HEAD_DIM)
    c64 = np.kron(np.eye(HEADS), np.cos(ang) * s)
    s64 = np.kron(np.eye(HEADS), np.sin(ang) * s)
    return tuple(jnp.asarray(t, F32).astype(BF16) for t in (eye / HEAD_DIM, c64, s64))


def kernel(x, c, ctx, c_ctx, w_ada, b_ada, norm1_g, norm2_g, w_in, gmlp_ws, gmlp_bs, sconv_w,
           lambda_q1, lambda_k1, lambda_q2, lambda_k2, subln_g, w_out, ffn_w_up, ffn_conv_w,
           ffn_w_down, final_g):
    B, N, D = x.shape
    L = ctx.shape[1]
    fc = D_FF // FFN_CHUNKS

    cc = jnp.zeros((ADA_ROWS, D), F32).at[:B].set(c).at[CTX_ROW].set(c_ctx)
    mod_all = _adaln(cc, w_ada, b_ada)

    rope_tabs = _rope_tables(N)
    bd, c64, s64 = _head_block_tables()
    dft_n = _dft_tables(N)
    dft_l = _dft_tables(L)

    xc = ctx
    for l in range(DEPTH):
        ctx_out = l < DEPTH - 1
        lam_init = 0.8 - 0.6 * math.exp(-0.3 * l)
        mod = mod_all[l].reshape(ADA_ROWS, 1, 6 * D)
        w_in_bf = w_in[l].astype(BF16)
        w_out_bf = w_out[l].astype(BF16)
        wa3 = ffn_w_up[l][:, :D_FF].astype(BF16).reshape(D, FFN_CHUNKS, fc).transpose(1, 0, 2)
        wb3 = ffn_w_up[l][:, D_FF:].astype(BF16).reshape(D, FFN_CHUNKS, fc).transpose(1, 0, 2)
        wd3 = ffn_w_down[l].astype(BF16).reshape(FFN_CHUNKS, fc, D)
        cw3 = ffn_conv_w[l].reshape(3, FFN_CHUNKS, fc).transpose(1, 0, 2)
        ws_stack = gmlp_ws[l].transpose(1, 0, 2).reshape(CHUNK, HEADS * CHUNK).astype(BF16)
        bias_full = jnp.repeat(gmlp_bs[l].T, HEAD_DIM, axis=1)
        lam_params = jnp.stack([lambda_q1[l], lambda_k1[l], lambda_q2[l], lambda_k2[l]])
        subln_row = jnp.tile(subln_g[l], HEADS).reshape(1, GROUP_W)
        fin = final_g if l == DEPTH - 1 else None

        z = _in_proj(x, norm1_g[l], mod, w_in_bf, rope_tabs, ctx=False, ts=512)
        zc = _in_proj(xc, norm1_g[l], mod, w_in_bf, None, ctx=True, ts=L)

        mabc = _mixers(z, ws_stack, bias_full, sconv_w[l], bd, c64, s64, *dft_n, ctx=False)
        md = _attention(z, [zc, z], lam_params, subln_row, bd, lam_init=lam_init, tq=256,
                        ctx=False)
        x = _out_proj(x, mabc, md, w_out_bf, mod, ctx=False, ts=512)
        x = _ffn(x, norm2_g[l], mod, wa3, wb3, cw3, wd3, fin, ctx=False, ts=512)

        if ctx_out:
            mabc_c = _mixers(zc, ws_stack, bias_full, sconv_w[l], bd, c64, s64, *dft_l, ctx=True)
            md_c = _attention(zc, [zc], lam_params, subln_row, bd, lam_init=lam_init, tq=L,
                              ctx=True)
            xc = _out_proj(xc, mabc_c, md_c, w_out_bf, mod, ctx=True, ts=L)
            xc = _ffn(xc, norm2_g[l], mod, wa3, wb3, cw3, wd3, None, ctx=True, ts=L)
    return x
```

```python
import functools
import math

import numpy as np
import jax
import jax.numpy as jnp
from jax import lax
from jax.experimental import pallas as pl
from jax.experimental.pallas import tpu as pltpu

D_MODEL = 1024
DEPTH = 2
GRID_W = 64
N_GROUPS = 4
GROUP_W = D_MODEL // N_GROUPS
HEADS = 4
HEAD_DIM = GROUP_W // HEADS
CHUNK = 128
DIFF_QK = HEAD_DIM // 2
ROT_HALF = DIFF_QK // 2
ROPE_THETA = 10000.0
D_FF = 2816
EPS = 1e-6
G_AU, G_AV, G_BX, G_BB, G_BC, G_CF, G_DQ, G_DK, G_DV = range(9)
IN_COLS = 9 * GROUP_W

F32 = jnp.float32
BF16 = jnp.bfloat16

ADA_ROWS = 24
CTX_ROW = 16
FFN_HALO = 16
MXU_K = 256
FFN_BOUNDS = ((0, 6 * MXU_K), (6 * MXU_K, D_FF))
VMEM_LIMIT = 56 * 1024 * 1024
QK_SCALE = DIFF_QK ** -0.5 * math.log2(math.e)
NT_DIMS = (((1,), (1,)), ((), ()))
ONES_ROWS = 16
V_ROWS = HEAD_DIM + ONES_ROWS


def _cparams(sem):
    return pltpu.CompilerParams(dimension_semantics=sem, vmem_limit_bytes=VMEM_LIMIT)


def _const_spec(shape):
    nd = len(shape)
    return pl.BlockSpec(shape, lambda *_: (0,) * nd, pipeline_mode=pl.Buffered(1))


def _rms(x):
    return x * lax.rsqrt(jnp.mean(x * x, axis=-1, keepdims=True) + EPS)


def _group_mean(sq, bd):
    hi = sq.astype(BF16)
    lo = (sq - hi.astype(F32)).astype(BF16)
    return (jnp.dot(hi, bd, preferred_element_type=F32)
            + jnp.dot(lo, bd, preferred_element_type=F32))


def _ada_body(c_ref, w_ref, b_ref, o_ref):
    s = jax.nn.silu(c_ref[...]).astype(BF16)
    o_ref[0] = jnp.dot(s, w_ref[0].astype(BF16), preferred_element_type=F32) + b_ref[0]


def _adaln(cc, w_ada, b_ada):
    n_col = 6
    return pl.pallas_call(
        _ada_body,
        grid=(DEPTH, n_col),
        in_specs=[
            pl.BlockSpec((ADA_ROWS, D_MODEL), lambda l, j: (0, 0)),
            pl.BlockSpec((1, D_MODEL, D_MODEL), lambda l, j: (l, 0, j)),
            pl.BlockSpec((1, 1, D_MODEL), lambda l, j: (l, 0, j)),
        ],
        out_specs=pl.BlockSpec((1, ADA_ROWS, D_MODEL), lambda l, j: (l, 0, j)),
        out_shape=jax.ShapeDtypeStruct((DEPTH, ADA_ROWS, 6 * D_MODEL), F32),
        compiler_params=_cparams(("parallel", "parallel")),
        name="adaln",
    )(cc, w_ada, b_ada.reshape(DEPTH, 1, 6 * D_MODEL))


def _in_body(x_ref, g_ref, sh_ref, sc_ref, w_ref, *rest, rope):
    if rope:
        cos_ref, sa_ref, sb_ref, z_ref = rest
    else:
        (z_ref,) = rest
    x = x_ref[0]
    h = _rms(x) * g_ref[...]
    h = h * (1.0 + sc_ref[0]) + sh_ref[0]
    z = jnp.dot(h.astype(BF16), w_ref[...], preferred_element_type=F32)
    q0 = G_DQ * GROUP_W
    z_ref[0, :, :q0] = z[:, :q0].astype(BF16)
    for g in (G_DQ, G_DK):
        t = z[:, g * GROUP_W:(g + 1) * GROUP_W]
        if rope:
            t = (t * cos_ref[...]
                 + pltpu.roll(t, GROUP_W - ROT_HALF // 2, axis=1) * sa_ref[...]
                 + pltpu.roll(t, ROT_HALF // 2, axis=1) * sb_ref[...])
        if g == G_DQ:
            t = t * QK_SCALE
        z_ref[0, :, g * GROUP_W:(g + 1) * GROUP_W] = t.astype(BF16)
    z_ref[0, :, G_DV * GROUP_W:] = z[:, G_DV * GROUP_W:].astype(BF16)


def _in_proj(x, g1, mod, w_in_bf, rope_tabs, *, ctx, ts):
    B, S, D = x.shape
    brow = (lambda b: CTX_ROW) if ctx else (lambda b: b)
    in_specs = [
        pl.BlockSpec((1, ts, D), lambda i, b: (b, i, 0)),
        _const_spec((1, D)),
        pl.BlockSpec((1, 1, D), lambda i, b: (brow(b), 0, 0)),
        pl.BlockSpec((1, 1, D), lambda i, b: (brow(b), 0, 1)),
        _const_spec((D, IN_COLS)),
    ]
    args = [x, g1.reshape(1, D), mod, mod, w_in_bf]
    rope = rope_tabs is not None
    if rope:
        in_specs += [pl.BlockSpec((ts, GROUP_W), lambda i, b: (i, 0))] * 3
        args += list(rope_tabs)
    return pl.pallas_call(
        functools.partial(_in_body, rope=rope),
        grid=(S // ts, B),
        in_specs=in_specs,
        out_specs=pl.BlockSpec((1, ts, IN_COLS), lambda i, b: (b, i, 0)),
        out_shape=jax.ShapeDtypeStruct((B, S, IN_COLS), BF16),
        compiler_params=_cparams(("parallel", "parallel")),
        name="in_proj_ctx" if ctx else "in_proj",
    )(*args)


def _mix_body(z_ref, ws_ref, bias_ref, sc_ref, bd_ref, c64_ref, s64_ref, wc_ref, wsn_ref,
              o_ref, v_s, *, seq):
    lane = lax.broadcasted_iota(jnp.int32, (1, GROUP_W), 1)
    head_of_lane = lax.shift_right_logical(lane, 6)
    ws = ws_ref[...]
    bias = bias_ref[...]

    g = jax.nn.gelu(z_ref[0, :, G_AV * GROUP_W:(G_AV + 1) * GROUP_W].astype(F32))
    v_s[...] = (g * lax.rsqrt(_group_mean(g * g, bd_ref[...]) + EPS)).astype(BF16)
    for c in range(seq // CHUNK):
        r = c * CHUNK
        m4 = jnp.dot(ws, v_s[r:r + CHUNK, :], preferred_element_type=F32)
        mixed = m4[0:CHUNK]
        for h in range(1, HEADS):
            mixed = jnp.where(head_of_lane == h, m4[h * CHUNK:(h + 1) * CHUNK], mixed)
        u = jax.nn.gelu(z_ref[0, r:r + CHUNK, G_AU * GROUP_W:(G_AU + 1) * GROUP_W].astype(F32))
        o_ref[0, r:r + CHUNK, 0:GROUP_W] = (u * (mixed + bias)).astype(BF16)

    bx = z_ref[0, :, G_BX * GROUP_W:(G_BX + 1) * GROUP_W].astype(F32)
    bb = z_ref[0, :, G_BB * GROUP_W:(G_BB + 1) * GROUP_W].astype(F32)
    bc = z_ref[0, :, G_BC * GROUP_W:(G_BC + 1) * GROUP_W].astype(F32)
    cx = bc * bx
    row = lax.broadcasted_iota(jnp.int32, (seq, 1), 0)
    prev = jnp.where(row == 0, 0.0, pltpu.roll(cx, 1, axis=0))
    nxt = jnp.where(row == seq - 1, 0.0, pltpu.roll(cx, seq - 1, axis=0))
    w = sc_ref[...]
    conv = prev * w[0:1] + cx * w[1:2] + nxt * w[2:3]
    o_ref[0, :, GROUP_W:2 * GROUP_W] = (bb * conv).astype(BF16)

    f = z_ref[0, :, G_CF * GROUP_W:(G_CF + 1) * GROUP_W]
    gc = jnp.dot(f, c64_ref[...], preferred_element_type=F32).astype(BF16)
    gs = jnp.dot(f, s64_ref[...], preferred_element_type=F32).astype(BF16)
    fo = (jnp.dot(wc_ref[...], gc, preferred_element_type=F32)
          + jnp.dot(wsn_ref[...], gs, preferred_element_type=F32))
    o_ref[0, :, 2 * GROUP_W:3 * GROUP_W] = fo.astype(BF16)


def _mixers(z, ws_rows, bias_full, sconv, bd, c64, s64, wc, wsn, *, ctx):
    B, S, _ = z.shape
    zc = (G_CF + 1) * GROUP_W
    return pl.pallas_call(
        functools.partial(_mix_body, seq=S),
        grid=(B,),
        in_specs=[
            pl.BlockSpec((1, S, zc), lambda b: (b, 0, 0)),
            _const_spec((HEADS * CHUNK, CHUNK)),
            _const_spec((CHUNK, GROUP_W)),
            _const_spec((3, GROUP_W)),
            _const_spec((GROUP_W, GROUP_W)),
            _const_spec((GROUP_W, GROUP_W)),
            _const_spec((GROUP_W, GROUP_W)),
            _const_spec((S, S)),
            _const_spec((S, S)),
        ],
        out_specs=pl.BlockSpec((1, S, 3 * GROUP_W), lambda b: (b, 0, 0)),
        out_shape=jax.ShapeDtypeStruct((B, S, 3 * GROUP_W), BF16),
        scratch_shapes=[pltpu.VMEM((S, GROUP_W), BF16)],
        compiler_params=_cparams(("parallel",)),
        name="mixers_ctx" if ctx else "mixers",
    )(z, ws_rows, bias_full, sconv, bd, c64, s64, wc, wsn)


def _lambda(lam_ref, lam_init):
    lp = lam_ref[...]
    return (jnp.exp(jnp.sum(lp[0:1] * lp[1:2], axis=-1, keepdims=True))
            - jnp.exp(jnp.sum(lp[2:3] * lp[3:4], axis=-1, keepdims=True)) + lam_init)


def _softmax_diff(s_maps, lam):
    es, cs = [], []
    for st in s_maps:
        e = jnp.exp2(st - jnp.max(st, axis=0, keepdims=True))
        es.append(e)
        cs.append(jnp.sum(e, axis=0, keepdims=True))
    return (es[0] * (1.0 / cs[0]) - es[1] * (lam / cs[1])).astype(BF16)


def _subln(ot, bd_ref, g_ref, lam_init):
    o = ot.T
    y = o * lax.rsqrt(_group_mean(o * o, bd_ref[...]) + EPS) * g_ref[...]
    return (y * (1.0 - lam_init)).astype(BF16)


def _attn_ctx_body(q_ref, k_ref, v_ref, lam_ref, g_ref, bd_ref, o_ref, *, lam_init):
    q = q_ref[0]
    kall = k_ref[0]
    vt = v_ref[0].astype(F32).T.astype(BF16)
    lam = _lambda(lam_ref, lam_init)
    lane = lax.broadcasted_iota(jnp.int32, (1, GROUP_W), 1)
    map_of_lane = lax.shift_right_logical(lane, 5)
    outs = []
    for h in range(HEADS):
        s_maps = [lax.dot_general(kall, jnp.where(map_of_lane == 2 * h + m, q, jnp.zeros_like(q)),
                                  NT_DIMS, preferred_element_type=F32) for m in range(2)]
        at = _softmax_diff(s_maps, lam)
        outs.append(jnp.dot(vt[h * HEAD_DIM:(h + 1) * HEAD_DIM, :], at,
                            preferred_element_type=F32))
    o_ref[0] = _subln(jnp.concatenate(outs, axis=0), bd_ref, g_ref, lam_init)


def _attention_ctx(zc, lam_params, subln_row, bd, *, lam_init):
    B, L, _ = zc.shape
    return pl.pallas_call(
        functools.partial(_attn_ctx_body, lam_init=lam_init),
        grid=(B,),
        in_specs=[pl.BlockSpec((1, L, GROUP_W), lambda b: (b, 0, G_DQ)),
                  pl.BlockSpec((1, L, GROUP_W), lambda b: (b, 0, G_DK)),
                  pl.BlockSpec((1, L, GROUP_W), lambda b: (b, 0, G_DV)),
                  _const_spec((4, DIFF_QK)), _const_spec((1, GROUP_W)),
                  _const_spec((GROUP_W, GROUP_W))],
        out_specs=pl.BlockSpec((1, L, GROUP_W), lambda b: (b, 0, 0)),
        out_shape=jax.ShapeDtypeStruct((B, L, GROUP_W), BF16),
        compiler_params=_cparams(("parallel",)),
        name="attn_ctx",
    )(zc, zc, zc, lam_params, subln_row, bd)


def _attn_body(q_ref, kc_ref, vc_ref, k_ref, v_ref, lam_ref, g_ref, bd_ref, o_ref,
               s_s, e_s, vt_s, ot_s, mx_s, *, lam_init, tq):
    L = kc_ref.shape[1]
    S = k_ref.shape[1]
    n_tiles = S // tq
    vt = jnp.concatenate([vc_ref[0].astype(F32).T, v_ref[0].astype(F32).T], axis=1)
    for h in range(HEADS):
        vt_s[h, 0:HEAD_DIM, :] = vt[h * HEAD_DIM:(h + 1) * HEAD_DIM, :].astype(BF16)
        vt_s[h, HEAD_DIM:, :] = jnp.ones((ONES_ROWS, L + S), BF16)
    lam = _lambda(lam_ref, lam_init)
    lane = lax.broadcasted_iota(jnp.int32, (1, GROUP_W), 1)
    map_of_lane = lax.shift_right_logical(lane, 5)

    def tile_rows(i):
        r = i * tq
        return pl.ds(r if isinstance(r, int) else pl.multiple_of(r, tq), tq)

    def scores(i, h):
        q = q_ref[0, tile_rows(i), :]
        for m in range(2):
            qm = jnp.where(map_of_lane == 2 * h + m, q, jnp.zeros_like(q))
            sc = lax.dot_general(kc_ref[0], qm, NT_DIMS, preferred_element_type=F32)
            sl = lax.dot_general(k_ref[0], qm, NT_DIMS, preferred_element_type=F32)
            s_s[h, m, 0:L, :] = sc
            s_s[h, m, L:, :] = sl
            mx_s[h, m, 0:1, :] = jnp.maximum(jnp.max(sc, axis=0, keepdims=True),
                                             jnp.max(sl, axis=0, keepdims=True))

    def exponentials(h):
        for m in range(2):
            e_s[h, m] = jnp.exp2(s_s[h, m] - mx_s[h, m, 0:1, :]).astype(BF16)

    def values(h):
        r1 = jnp.dot(vt_s[h], e_s[h, 0], preferred_element_type=F32)
        r2 = jnp.dot(vt_s[h], e_s[h, 1], preferred_element_type=F32)
        ot_s[h * HEAD_DIM:(h + 1) * HEAD_DIM, :] = (
            r1[0:HEAD_DIM] * (1.0 / r1[HEAD_DIM:HEAD_DIM + 1])
            - r2[0:HEAD_DIM] * (lam / r2[HEAD_DIM:HEAD_DIM + 1]))

    def finish(i):
        o_ref[0, tile_rows(i), :] = _subln(ot_s[...], bd_ref, g_ref, lam_init)

    def tile(i, first):
        scores(i, 0)
        if not first:
            exponentials(3)
            values(2)
        scores(i, 1)
        exponentials(0)
        if not first:
            values(3)
            finish(i - 1)
        scores(i, 2)
        exponentials(1)
        values(0)
        scores(i, 3)
        exponentials(2)
        values(1)

    tile(0, True)

    def loop_body(i, carry):
        tile(i, False)
        return carry

    lax.fori_loop(1, n_tiles, loop_body, 0)
    exponentials(3)
    values(2)
    values(3)
    finish(n_tiles - 1)


def _attention(z, zc, lam_params, subln_row, bd, *, lam_init, tq):
    B, S, _ = z.shape
    L = zc.shape[1]
    nk = L + S
    return pl.pallas_call(
        functools.partial(_attn_body, lam_init=lam_init, tq=tq),
        grid=(B,),
        in_specs=[pl.BlockSpec((1, S, GROUP_W), lambda b: (b, 0, G_DQ)),
                  pl.BlockSpec((1, L, GROUP_W), lambda b: (b, 0, G_DK)),
                  pl.BlockSpec((1, L, GROUP_W), lambda b: (b, 0, G_DV)),
                  pl.BlockSpec((1, S, GROUP_W), lambda b: (b, 0, G_DK)),
                  pl.BlockSpec((1, S, GROUP_W), lambda b: (b, 0, G_DV)),
                  _const_spec((4, DIFF_QK)), _const_spec((1, GROUP_W)),
                  _const_spec((GROUP_W, GROUP_W))],
        out_specs=pl.BlockSpec((1, S, GROUP_W), lambda b: (b, 0, 0)),
        out_shape=jax.ShapeDtypeStruct((B, S, GROUP_W), BF16),
        scratch_shapes=[pltpu.VMEM((HEADS, 2, nk, tq), F32),
                        pltpu.VMEM((HEADS, 2, nk, tq), BF16),
                        pltpu.VMEM((HEADS, V_ROWS, nk), BF16),
                        pltpu.VMEM((GROUP_W, tq), F32),
                        pltpu.VMEM((HEADS, 2, 8, tq), F32)],
        compiler_params=_cparams(("parallel",)),
        name="attn",
    )(z, zc, zc, z, z, lam_params, subln_row, bd)


def _out_body(x_ref, mabc_ref, md_ref, w_ref, gt_ref, o_ref):
    k_abc = 3 * GROUP_W
    y = (jnp.dot(mabc_ref[0], w_ref[:k_abc, :], preferred_element_type=F32)
         + jnp.dot(md_ref[0], w_ref[k_abc:, :], preferred_element_type=F32))
    o_ref[0] = x_ref[0] + gt_ref[0] * y


def _out_proj(x, mabc, md, w_out_bf, mod, *, ctx, ts):
    B, S, D = x.shape
    brow = (lambda b: CTX_ROW) if ctx else (lambda b: b)
    return pl.pallas_call(
        _out_body,
        grid=(S // ts, B),
        in_specs=[
            pl.BlockSpec((1, ts, D), lambda i, b: (b, i, 0)),
            pl.BlockSpec((1, ts, 3 * GROUP_W), lambda i, b: (b, i, 0)),
            pl.BlockSpec((1, ts, GROUP_W), lambda i, b: (b, i, 0)),
            _const_spec((D, D)),
            pl.BlockSpec((1, 1, D), lambda i, b: (brow(b), 0, 2)),
        ],
        out_specs=pl.BlockSpec((1, ts, D), lambda i, b: (b, i, 0)),
        out_shape=jax.ShapeDtypeStruct((B, S, D), F32),
        compiler_params=_cparams(("parallel", "parallel")),
        name="out_proj_ctx" if ctx else "out_proj",
    )(x, mabc, md, w_out_bf, mod)


def _ffn_body(x_ref, xp_ref, xn_ref, g_ref, sh_ref, sc_ref, gt_ref, wu_ref, cw_ref, wd_ref,
              *rest, ts, final):
    if final:
        fg_ref, o_ref, h_s, a_s = rest
    else:
        o_ref, h_s, a_s = rest
    i = pl.program_id(1)
    last = pl.num_programs(1) - 1
    g = g_ref[...]
    sh = sh_ref[0]
    sc = sc_ref[0]

    def norm2(t):
        return (_rms(t) * g) * (1.0 + sc) + sh

    x = x_ref[0]
    H = FFN_HALO
    h_s[0:H, :] = jnp.where(i > 0, norm2(xp_ref[0]), 0.0).astype(BF16)
    h_s[H:H + ts, :] = norm2(x).astype(BF16)
    h_s[H + ts:, :] = jnp.where(i < last, norm2(xn_ref[0]), 0.0).astype(BF16)

    acc = jnp.zeros((ts, D_MODEL), F32)
    for c0, c1 in FFN_BOUNDS:
        w = c1 - c0
        a_s[:, 0:w] = jnp.dot(h_s[...], wu_ref[:, c0:c1], preferred_element_type=F32)
        cw = cw_ref[:, c0:c1]
        conv = (a_s[pl.ds(H - 1, ts), 0:w] * cw[0:1]
                + a_s[pl.ds(H, ts), 0:w] * cw[1:2]
                + a_s[pl.ds(H + 1, ts), 0:w] * cw[2:3])
        b = jnp.dot(h_s[H:H + ts, :], wu_ref[:, D_FF + c0:D_FF + c1],
                    preferred_element_type=F32)
        hid = (jax.nn.silu(conv) * b).astype(BF16)
        acc = acc + jnp.dot(hid, wd_ref[c0:c1, :], preferred_element_type=F32)
    y = x + gt_ref[0] * acc
    if final:
        y = _rms(y) * fg_ref[...]
    o_ref[0] = y


def _ffn(x, g2, mod, w_up_bf, conv_w, w_down_bf, final_g, *, ctx, ts):
    B, S, D = x.shape
    brow = (lambda b: CTX_ROW) if ctx else (lambda b: b)
    H = FFN_HALO
    nh = S // H
    per = ts // H
    wmax = max(c1 - c0 for c0, c1 in FFN_BOUNDS)
    final = final_g is not None
    in_specs = [
        pl.BlockSpec((1, ts, D), lambda b, i: (b, i, 0)),
        pl.BlockSpec((1, H, D), lambda b, i: (b, jnp.maximum(i * per - 1, 0), 0)),
        pl.BlockSpec((1, H, D), lambda b, i: (b, jnp.minimum((i + 1) * per, nh - 1), 0)),
        _const_spec((1, D)),
        pl.BlockSpec((1, 1, D), lambda b, i: (brow(b), 0, 3)),
        pl.BlockSpec((1, 1, D), lambda b, i: (brow(b), 0, 4)),
        pl.BlockSpec((1, 1, D), lambda b, i: (brow(b), 0, 5)),
        _const_spec((D, 2 * D_FF)),
        _const_spec((3, D_FF)),
        _const_spec((D_FF, D)),
    ]
    args = [x, x, x, g2.reshape(1, D), mod, mod, mod, w_up_bf, conv_w, w_down_bf]
    if final:
        in_specs.append(_const_spec((1, D)))
        args.append(final_g.reshape(1, D))
    return pl.pallas_call(
        functools.partial(_ffn_body, ts=ts, final=final),
        grid=(B, S // ts),
        in_specs=in_specs,
        out_specs=pl.BlockSpec((1, ts, D), lambda b, i: (b, i, 0)),
        out_shape=jax.ShapeDtypeStruct((B, S, D), F32),
        scratch_shapes=[pltpu.VMEM((ts + 2 * H, D), BF16),
                        pltpu.VMEM((ts + 2 * H, wmax), F32)],
        compiler_params=_cparams(("parallel", "parallel")),
        name="ffn_ctx" if ctx else "ffn",
    )(*args)


def _rope_tables(n):
    pos = jnp.arange(n, dtype=jnp.int32)
    row = (pos // GRID_W).astype(F32)
    col = (pos % GRID_W).astype(F32)
    inv_freq = ROPE_THETA ** (-jnp.arange(0, ROT_HALF, 2, dtype=F32) / ROT_HALF)
    ang_r = row[:, None] * inv_freq
    ang_c = col[:, None] * inv_freq
    lane = np.arange(GROUP_W)
    axis = (lane // ROT_HALF) % 2
    half = (lane // (ROT_HALF // 2)) % 2
    j = lane % (ROT_HALF // 2)
    ang = jnp.where(jnp.asarray(axis == 0)[None, :], ang_r[:, j], ang_c[:, j])
    cos = jnp.cos(ang)
    sin = jnp.sin(ang)
    sa = jnp.where(jnp.asarray(half == 0)[None, :], -sin, 0.0)
    sb = jnp.where(jnp.asarray(half == 1)[None, :], sin, 0.0)
    return cos, sa, sb


def _dft_tables(n):
    k = jnp.arange(n, dtype=jnp.int32)
    kn = (k[:, None] * k[None, :]) % n
    ang = kn.astype(F32) * (2.0 * math.pi / n)
    s = 1.0 / math.sqrt(n)
    return (jnp.cos(ang) * s).astype(BF16), (jnp.sin(ang) * (-s)).astype(BF16)


def _head_block_tables():
    eye = np.kron(np.eye(HEADS), np.ones((HEAD_DIM, HEAD_DIM)))
    k = np.arange(HEAD_DIM)
    ang = 2.0 * np.pi * ((k[:, None] * k[None, :]) % HEAD_DIM) / HEAD_DIM
    s = 1.0 / math.sqrt(HEAD_DIM)
    c64 = np.kron(np.eye(HEADS), np.cos(ang) * s)
    s64 = np.kron(np.eye(HEADS), np.sin(ang) * s)
    return tuple(jnp.asarray(t, F32).astype(BF16) for t in (eye / HEAD_DIM, c64, s64))


def kernel(x, c, ctx, c_ctx, w_ada, b_ada, norm1_g, norm2_g, w_in, gmlp_ws, gmlp_bs, sconv_w,
           lambda_q1, lambda_k1, lambda_q2, lambda_k2, subln_g, w_out, ffn_w_up, ffn_conv_w,
           ffn_w_down, final_g):
    B, N, D = x.shape
    L = ctx.shape[1]

    cc = jnp.zeros((ADA_ROWS, D), F32).at[:B].set(c).at[CTX_ROW].set(c_ctx)
    mod_all = _adaln(cc, w_ada, b_ada)

    rope_tabs = _rope_tables(N)
    bd, c64, s64 = _head_block_tables()
    dft_n = _dft_tables(N)
    dft_l = _dft_tables(L)

    xc = ctx
    for l in range(DEPTH):
        ctx_out = l < DEPTH - 1
        lam_init = 0.8 - 0.6 * math.exp(-0.3 * l)
        mod = mod_all[l].reshape(ADA_ROWS, 1, 6 * D)
        w_in_bf = w_in[l].astype(BF16)
        w_out_bf = w_out[l].astype(BF16)
        w_up_bf = ffn_w_up[l].astype(BF16)
        w_down_bf = ffn_w_down[l].astype(BF16)
        ws_rows = gmlp_ws[l].reshape(HEADS * CHUNK, CHUNK).astype(BF16)
        bias_full = jnp.repeat(gmlp_bs[l].T, HEAD_DIM, axis=1)
        lam_params = jnp.stack([lambda_q1[l], lambda_k1[l], lambda_q2[l], lambda_k2[l]])
        subln_row = jnp.tile(subln_g[l], HEADS).reshape(1, GROUP_W)
        fin = final_g if l == DEPTH - 1 else None

        z = _in_proj(x, norm1_g[l], mod, w_in_bf, rope_tabs, ctx=False, ts=512)
        zc = _in_proj(xc, norm1_g[l], mod, w_in_bf, None, ctx=True, ts=L)

        mabc = _mixers(z, ws_rows, bias_full, sconv_w[l], bd, c64, s64, *dft_n, ctx=False)
        md = _attention(z, zc, lam_params, subln_row, bd, lam_init=lam_init, tq=256)
        x = _out_proj(x, mabc, md, w_out_bf, mod, ctx=False, ts=512)
        x = _ffn(x, norm2_g[l], mod, w_up_bf, ffn_conv_w[l], w_down_bf, fin, ctx=False, ts=512)

        if ctx_out:
            mabc_c = _mixers(zc, ws_rows, bias_full, sconv_w[l], bd, c64, s64, *dft_l, ctx=True)
            md_c = _attention_ctx(zc, lam_params, subln_row, bd, lam_init=lam_init)
            xc = _out_proj(xc, mabc_c, md_c, w_out_bf, mod, ctx=True, ts=L)
            xc = _ffn(xc, norm2_g[l], mod, w_up_bf, ffn_conv_w[l], w_down_bf, None, ctx=True,
                      ts=L)
    return x
```

```python
import functools
import math

import numpy as np
import jax
import jax.numpy as jnp
from jax import lax
from jax.experimental import pallas as pl
from jax.experimental.pallas import tpu as pltpu

D_MODEL = 1024
DEPTH = 2
GRID_W = 64
N_GROUPS = 4
GROUP_W = D_MODEL // N_GROUPS
HEADS = 4
HEAD_DIM = GROUP_W // HEADS
CHUNK = 128
DIFF_QK = HEAD_DIM // 2
ROT_HALF = DIFF_QK // 2
ROPE_THETA = 10000.0
D_FF = 2816
EPS = 1e-6
G_AU, G_AV, G_BX, G_BB, G_BC, G_CF, G_DQ, G_DK, G_DV = range(9)
IN_COLS = 9 * GROUP_W

F32 = jnp.float32
BF16 = jnp.bfloat16

ADA_ROWS = 24
CTX_ROW = 16
FFN_HALO = 16
MXU_K = 256
FFN_BOUNDS = ((0, 6 * MXU_K), (6 * MXU_K, D_FF))
VMEM_LIMIT = 56 * 1024 * 1024
QK_SCALE = DIFF_QK ** -0.5 * math.log2(math.e)
NT_DIMS = (((1,), (1,)), ((), ()))
IN_SUB = 256
ONES_ROWS = 16
V_ROWS = HEAD_DIM + ONES_ROWS


def _cparams(sem):
    return pltpu.CompilerParams(dimension_semantics=sem, vmem_limit_bytes=VMEM_LIMIT)


def _const_spec(shape):
    nd = len(shape)
    return pl.BlockSpec(shape, lambda *_: (0,) * nd, pipeline_mode=pl.Buffered(1))


def _rms(x):
    return x * lax.rsqrt(jnp.mean(x * x, axis=-1, keepdims=True) + EPS)


def _group_mean(sq, bd):
    hi = sq.astype(BF16)
    lo = (sq - hi.astype(F32)).astype(BF16)
    return (jnp.dot(hi, bd, preferred_element_type=F32)
            + jnp.dot(lo, bd, preferred_element_type=F32))


def _ada_body(c_ref, w_ref, b_ref, o_ref):
    s = jax.nn.silu(c_ref[...]).astype(BF16)
    o_ref[0] = jnp.dot(s, w_ref[0].astype(BF16), preferred_element_type=F32) + b_ref[0]


def _adaln(cc, w_ada, b_ada):
    n_col = 6
    return pl.pallas_call(
        _ada_body,
        grid=(DEPTH, n_col),
        in_specs=[
            pl.BlockSpec((ADA_ROWS, D_MODEL), lambda l, j: (0, 0)),
            pl.BlockSpec((1, D_MODEL, D_MODEL), lambda l, j: (l, 0, j)),
            pl.BlockSpec((1, 1, D_MODEL), lambda l, j: (l, 0, j)),
        ],
        out_specs=pl.BlockSpec((1, ADA_ROWS, D_MODEL), lambda l, j: (l, 0, j)),
        out_shape=jax.ShapeDtypeStruct((DEPTH, ADA_ROWS, 6 * D_MODEL), F32),
        compiler_params=_cparams(("parallel", "parallel")),
        name="adaln",
    )(cc, w_ada, b_ada.reshape(DEPTH, 1, 6 * D_MODEL))


def _in_body(x_ref, g_ref, sh_ref, sc_ref, w_ref, *rest, rope):
    if rope:
        cos_ref, sa_ref, sb_ref, z_ref = rest
    else:
        (z_ref,) = rest
    ts = x_ref.shape[1]
    sub = min(ts, IN_SUB)
    q0 = G_DQ * GROUP_W
    for r in range(0, ts, sub):
        rows = slice(r, r + sub)
        h = _rms(x_ref[0, rows, :]) * g_ref[...]
        h = h * (1.0 + sc_ref[0]) + sh_ref[0]
        z = jnp.dot(h.astype(BF16), w_ref[...], preferred_element_type=F32)
        z_ref[0, rows, :q0] = z[:, :q0].astype(BF16)
        for g in (G_DQ, G_DK):
            t = z[:, g * GROUP_W:(g + 1) * GROUP_W]
            if rope:
                t = (t * cos_ref[rows, :]
                     + pltpu.roll(t, GROUP_W - ROT_HALF // 2, axis=1) * sa_ref[rows, :]
                     + pltpu.roll(t, ROT_HALF // 2, axis=1) * sb_ref[rows, :])
            if g == G_DQ:
                t = t * QK_SCALE
            z_ref[0, rows, g * GROUP_W:(g + 1) * GROUP_W] = t.astype(BF16)
        z_ref[0, rows, G_DV * GROUP_W:] = z[:, G_DV * GROUP_W:].astype(BF16)


def _in_proj(x, g1, mod, w_in_bf, rope_tabs, *, ctx, ts):
    B, S, D = x.shape
    brow = (lambda b: CTX_ROW) if ctx else (lambda b: b)
    in_specs = [
        pl.BlockSpec((1, ts, D), lambda i, b: (b, i, 0)),
        _const_spec((1, D)),
        pl.BlockSpec((1, 1, D), lambda i, b: (brow(b), 0, 0)),
        pl.BlockSpec((1, 1, D), lambda i, b: (brow(b), 0, 1)),
        _const_spec((D, IN_COLS)),
    ]
    args = [x, g1.reshape(1, D), mod, mod, w_in_bf]
    rope = rope_tabs is not None
    if rope:
        in_specs += [pl.BlockSpec((ts, GROUP_W), lambda i, b: (i, 0))] * 3
        args += list(rope_tabs)
    return pl.pallas_call(
        functools.partial(_in_body, rope=rope),
        grid=(S // ts, B),
        in_specs=in_specs,
        out_specs=pl.BlockSpec((1, ts, IN_COLS), lambda i, b: (b, i, 0)),
        out_shape=jax.ShapeDtypeStruct((B, S, IN_COLS), BF16),
        compiler_params=_cparams(("parallel", "parallel")),
        name="in_proj_ctx" if ctx else "in_proj",
    )(*args)


def _mix_body(z_ref, ws_ref, bias_ref, sc_ref, bd_ref, c64_ref, s64_ref, wc_ref, wsn_ref,
              o_ref, v_s, *, seq):
    lane = lax.broadcasted_iota(jnp.int32, (1, GROUP_W), 1)
    head_of_lane = lax.shift_right_logical(lane, 6)
    ws = ws_ref[...]
    bias = bias_ref[...]

    g = jax.nn.gelu(z_ref[0, :, G_AV * GROUP_W:(G_AV + 1) * GROUP_W].astype(F32))
    v_s[...] = (g * lax.rsqrt(_group_mean(g * g, bd_ref[...]) + EPS)).astype(BF16)
    for c in range(seq // CHUNK):
        r = c * CHUNK
        m4 = jnp.dot(ws, v_s[r:r + CHUNK, :], preferred_element_type=F32)
        mixed = m4[0:CHUNK]
        for h in range(1, HEADS):
            mixed = jnp.where(head_of_lane == h, m4[h * CHUNK:(h + 1) * CHUNK], mixed)
        u = jax.nn.gelu(z_ref[0, r:r + CHUNK, G_AU * GROUP_W:(G_AU + 1) * GROUP_W].astype(F32))
        o_ref[0, r:r + CHUNK, 0:GROUP_W] = (u * (mixed + bias)).astype(BF16)

    bx = z_ref[0, :, G_BX * GROUP_W:(G_BX + 1) * GROUP_W].astype(F32)
    bb = z_ref[0, :, G_BB * GROUP_W:(G_BB + 1) * GROUP_W].astype(F32)
    bc = z_ref[0, :, G_BC * GROUP_W:(G_BC + 1) * GROUP_W].astype(F32)
    cx = bc * bx
    row = lax.broadcasted_iota(jnp.int32, (seq, 1), 0)
    prev = jnp.where(row == 0, 0.0, pltpu.roll(cx, 1, axis=0))
    nxt = jnp.where(row == seq - 1, 0.0, pltpu.roll(cx, seq - 1, axis=0))
    w = sc_ref[...]
    conv = prev * w[0:1] + cx * w[1:2] + nxt * w[2:3]
    o_ref[0, :, GROUP_W:2 * GROUP_W] = (bb * conv).astype(BF16)

    f = z_ref[0, :, G_CF * GROUP_W:(G_CF + 1) * GROUP_W]
    gc = jnp.dot(f, c64_ref[...], preferred_element_type=F32).astype(BF16)
    gs = jnp.dot(f, s64_ref[...], preferred_element_type=F32).astype(BF16)
    fo = (jnp.dot(wc_ref[...], gc, preferred_element_type=F32)
          + jnp.dot(wsn_ref[...], gs, preferred_element_type=F32))
    o_ref[0, :, 2 * GROUP_W:3 * GROUP_W] = fo.astype(BF16)


def _mixers(z, ws_rows, bias_full, sconv, bd, c64, s64, wc, wsn, *, ctx):
    B, S, _ = z.shape
    zc = (G_CF + 1) * GROUP_W
    return pl.pallas_call(
        functools.partial(_mix_body, seq=S),
        grid=(B,),
        in_specs=[
            pl.BlockSpec((1, S, zc), lambda b: (b, 0, 0)),
            _const_spec((HEADS * CHUNK, CHUNK)),
            _const_spec((CHUNK, GROUP_W)),
            _const_spec((3, GROUP_W)),
            _const_spec((GROUP_W, GROUP_W)),
            _const_spec((GROUP_W, GROUP_W)),
            _const_spec((GROUP_W, GROUP_W)),
            _const_spec((S, S)),
            _const_spec((S, S)),
        ],
        out_specs=pl.BlockSpec((1, S, 3 * GROUP_W), lambda b: (b, 0, 0)),
        out_shape=jax.ShapeDtypeStruct((B, S, 3 * GROUP_W), BF16),
        scratch_shapes=[pltpu.VMEM((S, GROUP_W), BF16)],
        compiler_params=_cparams(("parallel",)),
        name="mixers_ctx" if ctx else "mixers",
    )(z, ws_rows, bias_full, sconv, bd, c64, s64, wc, wsn)


def _lambda(lam_ref, lam_init):
    lp = lam_ref[...]
    return (jnp.exp(jnp.sum(lp[0:1] * lp[1:2], axis=-1, keepdims=True))
            - jnp.exp(jnp.sum(lp[2:3] * lp[3:4], axis=-1, keepdims=True)) + lam_init)


def _softmax_diff(s_maps, lam):
    es, cs = [], []
    for st in s_maps:
        e = jnp.exp2(st - jnp.max(st, axis=0, keepdims=True))
        es.append(e)
        cs.append(jnp.sum(e, axis=0, keepdims=True))
    return (es[0] * (1.0 / cs[0]) - es[1] * (lam / cs[1])).astype(BF16)


def _subln(ot, bd_ref, g_ref, lam_init):
    o = ot.T
    y = o * lax.rsqrt(_group_mean(o * o, bd_ref[...]) + EPS) * g_ref[...]
    return (y * (1.0 - lam_init)).astype(BF16)


def _attn_ctx_body(q_ref, k_ref, v_ref, lam_ref, g_ref, bd_ref, o_ref, *, lam_init):
    q = q_ref[0]
    kall = k_ref[0]
    vt = v_ref[0].astype(F32).T.astype(BF16)
    lam = _lambda(lam_ref, lam_init)
    lane = lax.broadcasted_iota(jnp.int32, (1, GROUP_W), 1)
    map_of_lane = lax.shift_right_logical(lane, 5)
    outs = []
    for h in range(HEADS):
        s_maps = [lax.dot_general(kall, jnp.where(map_of_lane == 2 * h + m, q, jnp.zeros_like(q)),
                                  NT_DIMS, preferred_element_type=F32) for m in range(2)]
        at = _softmax_diff(s_maps, lam)
        outs.append(jnp.dot(vt[h * HEAD_DIM:(h + 1) * HEAD_DIM, :], at,
                            preferred_element_type=F32))
    o_ref[0] = _subln(jnp.concatenate(outs, axis=0), bd_ref, g_ref, lam_init)


def _attention_ctx(zc, lam_params, subln_row, bd, *, lam_init):
    B, L, _ = zc.shape
    return pl.pallas_call(
        functools.partial(_attn_ctx_body, lam_init=lam_init),
        grid=(B,),
        in_specs=[pl.BlockSpec((1, L, GROUP_W), lambda b: (b, 0, G_DQ)),
                  pl.BlockSpec((1, L, GROUP_W), lambda b: (b, 0, G_DK)),
                  pl.BlockSpec((1, L, GROUP_W), lambda b: (b, 0, G_DV)),
                  _const_spec((4, DIFF_QK)), _const_spec((1, GROUP_W)),
                  _const_spec((GROUP_W, GROUP_W))],
        out_specs=pl.BlockSpec((1, L, GROUP_W), lambda b: (b, 0, 0)),
        out_shape=jax.ShapeDtypeStruct((B, L, GROUP_W), BF16),
        compiler_params=_cparams(("parallel",)),
        name="attn_ctx",
    )(zc, zc, zc, lam_params, subln_row, bd)


def _attn_body(q_ref, kc_ref, vc_ref, k_ref, v_ref, lam_ref, g_ref, bd_ref, o_ref,
               s_s, vt_s, ot_s, mx_s, *, lam_init, tq):
    L = kc_ref.shape[1]
    S = k_ref.shape[1]
    n_tiles = S // tq
    vt = jnp.concatenate([vc_ref[0].astype(F32).T, v_ref[0].astype(F32).T], axis=1)
    for h in range(HEADS):
        vt_s[h, 0:HEAD_DIM, :] = vt[h * HEAD_DIM:(h + 1) * HEAD_DIM, :].astype(BF16)
        vt_s[h, HEAD_DIM:, :] = jnp.ones((ONES_ROWS, L + S), BF16)
    lam = _lambda(lam_ref, lam_init)
    lane = lax.broadcasted_iota(jnp.int32, (1, GROUP_W), 1)
    map_of_lane = lax.shift_right_logical(lane, 5)

    def tile_rows(i):
        r = i * tq
        return pl.ds(r if isinstance(r, int) else pl.multiple_of(r, tq), tq)

    def scores(i, h):
        q = q_ref[0, tile_rows(i), :]
        for m in range(2):
            qm = jnp.where(map_of_lane == 2 * h + m, q, jnp.zeros_like(q))
            sc = lax.dot_general(kc_ref[0], qm, NT_DIMS, preferred_element_type=F32)
            sl = lax.dot_general(k_ref[0], qm, NT_DIMS, preferred_element_type=F32)
            s_s[h, m, 0:L, :] = sc
            s_s[h, m, L:, :] = sl
            mx_s[h, m, 0:1, :] = jnp.maximum(jnp.max(sc, axis=0, keepdims=True),
                                             jnp.max(sl, axis=0, keepdims=True))

    def values(h):
        e1 = jnp.exp2(s_s[h, 0] - mx_s[h, 0, 0:1, :]).astype(BF16)
        e2 = jnp.exp2(s_s[h, 1] - mx_s[h, 1, 0:1, :]).astype(BF16)
        r1 = jnp.dot(vt_s[h], e1, preferred_element_type=F32)
        r2 = jnp.dot(vt_s[h], e2, preferred_element_type=F32)
        ot_s[h * HEAD_DIM:(h + 1) * HEAD_DIM, :] = (
            r1[0:HEAD_DIM] * (1.0 / r1[HEAD_DIM:HEAD_DIM + 1])
            - r2[0:HEAD_DIM] * (lam / r2[HEAD_DIM:HEAD_DIM + 1]))

    def finish(i):
        o_ref[0, tile_rows(i), :] = _subln(ot_s[...], bd_ref, g_ref, lam_init)

    def tile(i, first):
        scores(i, 0)
        if not first:
            values(3)
            finish(i - 1)
        scores(i, 1)
        values(0)
        scores(i, 2)
        values(1)
        scores(i, 3)
        values(2)

    tile(0, True)

    def loop_body(i, carry):
        tile(i, False)
        return carry

    lax.fori_loop(1, n_tiles, loop_body, 0)
    values(3)
    finish(n_tiles - 1)


def _attention(z, zc, lam_params, subln_row, bd, *, lam_init, tq):
    B, S, _ = z.shape
    L = zc.shape[1]
    nk = L + S
    return pl.pallas_call(
        functools.partial(_attn_body, lam_init=lam_init, tq=tq),
        grid=(B,),
        in_specs=[pl.BlockSpec((1, S, GROUP_W), lambda b: (b, 0, G_DQ)),
                  pl.BlockSpec((1, L, GROUP_W), lambda b: (b, 0, G_DK)),
                  pl.BlockSpec((1, L, GROUP_W), lambda b: (b, 0, G_DV)),
                  pl.BlockSpec((1, S, GROUP_W), lambda b: (b, 0, G_DK)),
                  pl.BlockSpec((1, S, GROUP_W), lambda b: (b, 0, G_DV)),
                  _const_spec((4, DIFF_QK)), _const_spec((1, GROUP_W)),
                  _const_spec((GROUP_W, GROUP_W))],
        out_specs=pl.BlockSpec((1, S, GROUP_W), lambda b: (b, 0, 0)),
        out_shape=jax.ShapeDtypeStruct((B, S, GROUP_W), BF16),
        scratch_shapes=[pltpu.VMEM((HEADS, 2, nk, tq), F32),
                        pltpu.VMEM((HEADS, V_ROWS, nk), BF16),
                        pltpu.VMEM((GROUP_W, tq), F32),
                        pltpu.VMEM((HEADS, 2, 8, tq), F32)],
        compiler_params=_cparams(("parallel",)),
        name="attn",
    )(z, zc, zc, z, z, lam_params, subln_row, bd)


def _tail_body(x_ref, xp_ref, xn_ref, ma_ref, map_ref, man_ref, md_ref, mdp_ref, mdn_ref,
               wo_ref, gt1_ref, g_ref, sh_ref, sc_ref, gt_ref, wu_ref, cw_ref, wd_ref,
               *rest, ts, final):
    if final:
        fg_ref, o_ref, x_s, m_s, h_s, a_s = rest
    else:
        o_ref, x_s, m_s, h_s, a_s = rest
    i = pl.program_id(1)
    last = pl.num_programs(1) - 1
    H = FFN_HALO
    k_abc = 3 * GROUP_W
    lo, mid, hi = slice(0, H), slice(H, H + ts), slice(H + ts, ts + 2 * H)
    for rows, xr, ar, dr in ((lo, xp_ref, map_ref, mdp_ref), (mid, x_ref, ma_ref, md_ref),
                             (hi, xn_ref, man_ref, mdn_ref)):
        x_s[rows, :] = xr[0]
        m_s[rows, 0:k_abc] = ar[0]
        m_s[rows, k_abc:] = dr[0]
    x1 = x_s[...] + gt1_ref[0] * jnp.dot(m_s[...], wo_ref[...], preferred_element_type=F32)
    x_s[...] = x1
    h = (_rms(x1) * g_ref[...]) * (1.0 + sc_ref[0]) + sh_ref[0]
    row = lax.broadcasted_iota(jnp.int32, (ts + 2 * H, 1), 0)
    inside = jnp.logical_and(jnp.logical_or(row >= H, i > 0),
                             jnp.logical_or(row < H + ts, i < last))
    h_s[...] = jnp.where(inside, h, 0.0).astype(BF16)

    acc = jnp.zeros((ts, D_MODEL), F32)
    for c0, c1 in FFN_BOUNDS:
        w = c1 - c0
        a_s[:, 0:w] = jnp.dot(h_s[...], wu_ref[:, c0:c1], preferred_element_type=F32)
        cw = cw_ref[:, c0:c1]
        conv = (a_s[pl.ds(H - 1, ts), 0:w] * cw[0:1]
                + a_s[pl.ds(H, ts), 0:w] * cw[1:2]
                + a_s[pl.ds(H + 1, ts), 0:w] * cw[2:3])
        b = jnp.dot(h_s[H:H + ts, :], wu_ref[:, D_FF + c0:D_FF + c1],
                    preferred_element_type=F32)
        hid = (jax.nn.silu(conv) * b).astype(BF16)
        acc = acc + jnp.dot(hid, wd_ref[c0:c1, :], preferred_element_type=F32)
    y = x_s[mid, :] + gt_ref[0] * acc
    if final:
        y = _rms(y) * fg_ref[...]
    o_ref[0] = y


def _tail(x, mabc, md, w_out_bf, g2, mod, w_up_bf, conv_w, w_down_bf, final_g, *, ctx, ts):
    B, S, D = x.shape
    brow = (lambda b: CTX_ROW) if ctx else (lambda b: b)
    H = FFN_HALO
    nh = S // H
    per = ts // H
    wmax = max(c1 - c0 for c0, c1 in FFN_BOUNDS)
    final = final_g is not None

    def tiled(width):
        return [
            pl.BlockSpec((1, ts, width), lambda b, i: (b, i, 0)),
            pl.BlockSpec((1, H, width), lambda b, i: (b, jnp.maximum(i * per - 1, 0), 0)),
            pl.BlockSpec((1, H, width), lambda b, i: (b, jnp.minimum((i + 1) * per, nh - 1), 0)),
        ]

    def mod_spec(col):
        return pl.BlockSpec((1, 1, D), lambda b, i: (brow(b), 0, col))

    in_specs = tiled(D) + tiled(3 * GROUP_W) + tiled(GROUP_W) + [
        _const_spec((D, D)),
        mod_spec(2),
        _const_spec((1, D)),
        mod_spec(3),
        mod_spec(4),
        mod_spec(5),
        _const_spec((D, 2 * D_FF)),
        _const_spec((3, D_FF)),
        _const_spec((D_FF, D)),
    ]
    args = [x, x, x, mabc, mabc, mabc, md, md, md, w_out_bf, mod, g2.reshape(1, D), mod, mod,
            mod, w_up_bf, conv_w, w_down_bf]
    if final:
        in_specs.append(_const_spec((1, D)))
        args.append(final_g.reshape(1, D))
    return pl.pallas_call(
        functools.partial(_tail_body, ts=ts, final=final),
        grid=(B, S // ts),
        in_specs=in_specs,
        out_specs=pl.BlockSpec((1, ts, D), lambda b, i: (b, i, 0)),
        out_shape=jax.ShapeDtypeStruct((B, S, D), F32),
        scratch_shapes=[pltpu.VMEM((ts + 2 * H, D), F32),
                        pltpu.VMEM((ts + 2 * H, D), BF16),
                        pltpu.VMEM((ts + 2 * H, D), BF16),
                        pltpu.VMEM((ts + 2 * H, wmax), F32)],
        compiler_params=_cparams(("parallel", "parallel")),
        name="tail_ctx" if ctx else "tail",
    )(*args)


def _rope_tables(n):
    pos = jnp.arange(n, dtype=jnp.int32)
    row = (pos // GRID_W).astype(F32)
    col = (pos % GRID_W).astype(F32)
    inv_freq = ROPE_THETA ** (-jnp.arange(0, ROT_HALF, 2, dtype=F32) / ROT_HALF)
    ang_r = row[:, None] * inv_freq
    ang_c = col[:, None] * inv_freq
    lane = np.arange(GROUP_W)
    axis = (lane // ROT_HALF) % 2
    half = (lane // (ROT_HALF // 2)) % 2
    j = lane % (ROT_HALF // 2)
    ang = jnp.where(jnp.asarray(axis == 0)[None, :], ang_r[:, j], ang_c[:, j])
    cos = jnp.cos(ang)
    sin = jnp.sin(ang)
    sa = jnp.where(jnp.asarray(half == 0)[None, :], -sin, 0.0)
    sb = jnp.where(jnp.asarray(half == 1)[None, :], sin, 0.0)
    return cos, sa, sb


def _dft_tables(n):
    n2c = GRID_W
    n1c = n // n2c
    k = jnp.arange(n, dtype=jnp.int32)

    def trig(cols, stride):
        kn = (k[:, None] * (jnp.arange(cols, dtype=jnp.int32) * stride)[None, :]) % n
        ang = kn.astype(F32) * (2.0 * math.pi / n)
        return jnp.cos(ang), jnp.sin(ang)

    ca, sa = trig(n1c, n2c)
    cb, sb = trig(n2c, 1)
    s = 1.0 / math.sqrt(n)
    cos = ca[:, :, None] * cb[:, None, :] - sa[:, :, None] * sb[:, None, :]
    sin = sa[:, :, None] * cb[:, None, :] + ca[:, :, None] * sb[:, None, :]
    return ((cos * s).reshape(n, n).astype(BF16), (sin * (-s)).reshape(n, n).astype(BF16))


def _head_block_tables():
    eye = np.kron(np.eye(HEADS), np.ones((HEAD_DIM, HEAD_DIM)))
    k = np.arange(HEAD_DIM)
    ang = 2.0 * np.pi * ((k[:, None] * k[None, :]) % HEAD_DIM) / HEAD_DIM
    s = 1.0 / math.sqrt(HEAD_DIM)
    c64 = np.kron(np.eye(HEADS), np.cos(ang) * s)
    s64 = np.kron(np.eye(HEADS), np.sin(ang) * s)
    return tuple(jnp.asarray(t, F32).astype(BF16) for t in (eye / HEAD_DIM, c64, s64))


def kernel(x, c, ctx, c_ctx, w_ada, b_ada, norm1_g, norm2_g, w_in, gmlp_ws, gmlp_bs, sconv_w,
           lambda_q1, lambda_k1, lambda_q2, lambda_k2, subln_g, w_out, ffn_w_up, ffn_conv_w,
           ffn_w_down, final_g):
    B, N, D = x.shape
    L = ctx.shape[1]

    cc = jnp.zeros((ADA_ROWS, D), F32).at[:B].set(c).at[CTX_ROW].set(c_ctx)
    mod_all = _adaln(cc, w_ada, b_ada)

    rope_tabs = _rope_tables(N)
    bd, c64, s64 = _head_block_tables()
    dft_n = _dft_tables(N)
    dft_l = _dft_tables(L)

    xc = ctx
    for l in range(DEPTH):
        ctx_out = l < DEPTH - 1
        lam_init = 0.8 - 0.6 * math.exp(-0.3 * l)
        mod = mod_all[l].reshape(ADA_ROWS, 1, 6 * D)
        w_in_bf = w_in[l].astype(BF16)
        w_out_bf = w_out[l].astype(BF16)
        w_up_bf = ffn_w_up[l].astype(BF16)
        w_down_bf = ffn_w_down[l].astype(BF16)
        ws_rows = gmlp_ws[l].reshape(HEADS * CHUNK, CHUNK).astype(BF16)
        bias_full = jnp.repeat(gmlp_bs[l].T, HEAD_DIM, axis=1)
        lam_params = jnp.stack([lambda_q1[l], lambda_k1[l], lambda_q2[l], lambda_k2[l]])
        subln_row = jnp.tile(subln_g[l], HEADS).reshape(1, GROUP_W)
        fin = final_g if l == DEPTH - 1 else None

        z = _in_proj(x, norm1_g[l], mod, w_in_bf, rope_tabs, ctx=False, ts=512)
        zc = _in_proj(xc, norm1_g[l], mod, w_in_bf, None, ctx=True, ts=L)

        mabc = _mixers(z, ws_rows, bias_full, sconv_w[l], bd, c64, s64, *dft_n, ctx=False)
        md = _attention(z, zc, lam_params, subln_row, bd, lam_init=lam_init, tq=256)
        x = _tail(x, mabc, md, w_out_bf, norm2_g[l], mod, w_up_bf, ffn_conv_w[l], w_down_bf,
                  fin, ctx=False, ts=512)

        if ctx_out:
            mabc_c = _mixers(zc, ws_rows, bias_full, sconv_w[l], bd, c64, s64, *dft_l, ctx=True)
            md_c = _attention_ctx(zc, lam_params, subln_row, bd, lam_init=lam_init)
            xc = _tail(xc, mabc_c, md_c, w_out_bf, norm2_g[l], mod, w_up_bf, ffn_conv_w[l],
                       w_down_bf, None, ctx=True, ts=L)
    return x
```

```python
import functools
import math

import numpy as np
import jax
import jax.numpy as jnp
from jax import lax
from jax.experimental import pallas as pl
from jax.experimental.pallas import tpu as pltpu

D_MODEL = 1024
DEPTH = 2
GRID_W = 64
N_GROUPS = 4
GROUP_W = D_MODEL // N_GROUPS
HEADS = 4
HEAD_DIM = GROUP_W // HEADS
CHUNK = 128
DIFF_QK = HEAD_DIM // 2
ROT_HALF = DIFF_QK // 2
ROPE_THETA = 10000.0
D_FF = 2816
EPS = 1e-6
G_AU, G_AV, G_BX, G_BB, G_BC, G_CF, G_DQ, G_DK, G_DV = range(9)
IN_COLS = 9 * GROUP_W

F32 = jnp.float32
BF16 = jnp.bfloat16

ADA_ROWS = 24
CTX_ROW = 16
FFN_HALO = 16
MXU_K = 256
FFN_BOUNDS = ((0, 6 * MXU_K), (6 * MXU_K, D_FF))
VMEM_LIMIT = 56 * 1024 * 1024
QK_SCALE = DIFF_QK ** -0.5 * math.log2(math.e)
NT_DIMS = (((1,), (1,)), ((), ()))
DFT_ROWS = 512
IN_SUB = 128
ONES_ROWS = 16
V_ROWS = HEAD_DIM + ONES_ROWS


def _cparams(sem):
    return pltpu.CompilerParams(dimension_semantics=sem, vmem_limit_bytes=VMEM_LIMIT)


def _const_spec(shape):
    nd = len(shape)
    return pl.BlockSpec(shape, lambda *_: (0,) * nd, pipeline_mode=pl.Buffered(1))


def _rms(x):
    return x * lax.rsqrt(jnp.mean(x * x, axis=-1, keepdims=True) + EPS)


def _group_mean(sq, bd):
    hi = sq.astype(BF16)
    lo = (sq - hi.astype(F32)).astype(BF16)
    return (jnp.dot(hi, bd, preferred_element_type=F32)
            + jnp.dot(lo, bd, preferred_element_type=F32))


def _ada_body(c_ref, w_ref, b_ref, o_ref):
    s = jax.nn.silu(c_ref[...]).astype(BF16)
    o_ref[0] = jnp.dot(s, w_ref[0].astype(BF16), preferred_element_type=F32) + b_ref[0]


def _adaln(cc, w_ada, b_ada):
    n_col = 6
    return pl.pallas_call(
        _ada_body,
        grid=(DEPTH, n_col),
        in_specs=[
            pl.BlockSpec((ADA_ROWS, D_MODEL), lambda l, j: (0, 0)),
            pl.BlockSpec((1, D_MODEL, D_MODEL), lambda l, j: (l, 0, j)),
            pl.BlockSpec((1, 1, D_MODEL), lambda l, j: (l, 0, j)),
        ],
        out_specs=pl.BlockSpec((1, ADA_ROWS, D_MODEL), lambda l, j: (l, 0, j)),
        out_shape=jax.ShapeDtypeStruct((DEPTH, ADA_ROWS, 6 * D_MODEL), F32),
        compiler_params=_cparams(("parallel", "parallel")),
        name="adaln",
    )(cc, w_ada, b_ada.reshape(DEPTH, 1, 6 * D_MODEL))


def _in_body(x_ref, g_ref, sh_ref, sc_ref, w_ref, *rest, rope):
    if rope:
        cos_ref, sa_ref, sb_ref, z_ref = rest
    else:
        (z_ref,) = rest
    ts = x_ref.shape[1]
    sub = min(ts, IN_SUB)
    q0 = G_DQ * GROUP_W
    for r in range(0, ts, sub):
        rows = slice(r, r + sub)
        h = _rms(x_ref[0, rows, :]) * g_ref[...]
        h = h * (1.0 + sc_ref[0]) + sh_ref[0]
        z = jnp.dot(h.astype(BF16), w_ref[...], preferred_element_type=F32)
        z_ref[0, rows, :q0] = z[:, :q0].astype(BF16)
        for g in (G_DQ, G_DK):
            t = z[:, g * GROUP_W:(g + 1) * GROUP_W]
            if rope:
                t = (t * cos_ref[rows, :]
                     + pltpu.roll(t, GROUP_W - ROT_HALF // 2, axis=1) * sa_ref[rows, :]
                     + pltpu.roll(t, ROT_HALF // 2, axis=1) * sb_ref[rows, :])
            if g == G_DQ:
                t = t * QK_SCALE
            z_ref[0, rows, g * GROUP_W:(g + 1) * GROUP_W] = t.astype(BF16)
        z_ref[0, rows, G_DV * GROUP_W:] = z[:, G_DV * GROUP_W:].astype(BF16)


def _in_proj(x, g1, mod, w_in_bf, rope_tabs, *, ctx, ts):
    B, S, D = x.shape
    brow = (lambda b: CTX_ROW) if ctx else (lambda b: b)
    in_specs = [
        pl.BlockSpec((1, ts, D), lambda i, b: (b, i, 0)),
        _const_spec((1, D)),
        pl.BlockSpec((1, 1, D), lambda i, b: (brow(b), 0, 0)),
        pl.BlockSpec((1, 1, D), lambda i, b: (brow(b), 0, 1)),
        _const_spec((D, IN_COLS)),
    ]
    args = [x, g1.reshape(1, D), mod, mod, w_in_bf]
    rope = rope_tabs is not None
    if rope:
        in_specs += [pl.BlockSpec((ts, GROUP_W), lambda i, b: (i, 0))] * 3
        args += list(rope_tabs)
    return pl.pallas_call(
        functools.partial(_in_body, rope=rope),
        grid=(S // ts, B),
        in_specs=in_specs,
        out_specs=pl.BlockSpec((1, ts, IN_COLS), lambda i, b: (b, i, 0)),
        out_shape=jax.ShapeDtypeStruct((B, S, IN_COLS), BF16),
        compiler_params=_cparams(("parallel", "parallel")),
        name="in_proj_ctx" if ctx else "in_proj",
    )(*args)


def _mix_body(z_ref, ws_ref, bias_ref, sc_ref, bd_ref, c64_ref, s64_ref, wc_ref, wsn_ref,
              o_ref, v_s, gc_s, gs_s, *, seq):
    lane = lax.broadcasted_iota(jnp.int32, (1, GROUP_W), 1)
    head_of_lane = lax.shift_right_logical(lane, 6)
    ws = ws_ref[...]
    bias = bias_ref[...]

    tile = min(seq, DFT_ROWS)
    n_t = seq // tile
    per = tile // CHUNK

    def zcols(rows, grp):
        return z_ref[0, rows, grp * GROUP_W:(grp + 1) * GROUP_W]

    def head_norm(rows):
        g = jax.nn.gelu(zcols(rows, G_AV).astype(F32))
        v_s[rows, :] = (g * lax.rsqrt(_group_mean(g * g, bd_ref[...]) + EPS)).astype(BF16)

    def gated_conv(t):
        rows = slice(t * tile, (t + 1) * tile)

        def cx_of(rs):
            return zcols(rs, G_BC).astype(F32) * zcols(rs, G_BX).astype(F32)

        cx = cx_of(rows)
        zero = jnp.zeros((1, GROUP_W), F32)
        before = cx_of(slice(t * tile - 16, t * tile))[15:16] if t > 0 else zero
        after = cx_of(slice((t + 1) * tile, (t + 1) * tile + 16))[0:1] if t < n_t - 1 else zero
        row = lax.broadcasted_iota(jnp.int32, (tile, 1), 0)
        prev = jnp.where(row == 0, before, pltpu.roll(cx, 1, axis=0))
        nxt = jnp.where(row == tile - 1, after, pltpu.roll(cx, tile - 1, axis=0))
        w = sc_ref[...]
        conv = prev * w[0:1] + cx * w[1:2] + nxt * w[2:3]
        o_ref[0, rows, GROUP_W:2 * GROUP_W] = (zcols(rows, G_BB).astype(F32) * conv).astype(BF16)

    def gate_chunk(c):
        r = c * CHUNK
        m4 = jnp.dot(ws, v_s[r:r + CHUNK, :], preferred_element_type=F32)
        mixed = m4[0:CHUNK]
        for h in range(1, HEADS):
            mixed = jnp.where(head_of_lane == h, m4[h * CHUNK:(h + 1) * CHUNK], mixed)
        u = jax.nn.gelu(z_ref[0, r:r + CHUNK, G_AU * GROUP_W:(G_AU + 1) * GROUP_W].astype(F32))
        o_ref[0, r:r + CHUNK, 0:GROUP_W] = (u * (mixed + bias)).astype(BF16)

    f = zcols(slice(None), G_CF)
    gc_s[...] = jnp.dot(f, c64_ref[...], preferred_element_type=F32).astype(BF16)
    gs_s[...] = jnp.dot(f, s64_ref[...], preferred_element_type=F32).astype(BF16)
    for t in range(n_t):
        rows = slice(t * tile, (t + 1) * tile)
        fo = (jnp.dot(wc_ref[rows, :], gc_s[...], preferred_element_type=F32)
              + jnp.dot(wsn_ref[rows, :], gs_s[...], preferred_element_type=F32))
        o_ref[0, rows, 2 * GROUP_W:3 * GROUP_W] = fo.astype(BF16)
        head_norm(rows)
        for c in range(t * per, (t + 1) * per):
            gate_chunk(c)
        gated_conv(t)


def _mixers(z, ws_rows, bias_full, sconv, bd, c64, s64, wc, wsn, *, ctx):
    B, S, _ = z.shape
    zc = (G_CF + 1) * GROUP_W
    return pl.pallas_call(
        functools.partial(_mix_body, seq=S),
        grid=(B,),
        in_specs=[
            pl.BlockSpec((1, S, zc), lambda b: (b, 0, 0)),
            _const_spec((HEADS * CHUNK, CHUNK)),
            _const_spec((CHUNK, GROUP_W)),
            _const_spec((3, GROUP_W)),
            _const_spec((GROUP_W, GROUP_W)),
            _const_spec((GROUP_W, GROUP_W)),
            _const_spec((GROUP_W, GROUP_W)),
            _const_spec((S, S)),
            _const_spec((S, S)),
        ],
        out_specs=pl.BlockSpec((1, S, 3 * GROUP_W), lambda b: (b, 0, 0)),
        out_shape=jax.ShapeDtypeStruct((B, S, 3 * GROUP_W), BF16),
        scratch_shapes=[pltpu.VMEM((S, GROUP_W), BF16)] * 3,
        compiler_params=_cparams(("parallel",)),
        name="mixers_ctx" if ctx else "mixers",
    )(z, ws_rows, bias_full, sconv, bd, c64, s64, wc, wsn)


def _lambda(lam_ref, lam_init):
    lp = lam_ref[...]
    return (jnp.exp(jnp.sum(lp[0:1] * lp[1:2], axis=-1, keepdims=True))
            - jnp.exp(jnp.sum(lp[2:3] * lp[3:4], axis=-1, keepdims=True)) + lam_init)


def _softmax_diff(s_maps, lam):
    es, cs = [], []
    for st in s_maps:
        e = jnp.exp2(st - jnp.max(st, axis=0, keepdims=True))
        es.append(e)
        cs.append(jnp.sum(e, axis=0, keepdims=True))
    return (es[0] * (1.0 / cs[0]) - es[1] * (lam / cs[1])).astype(BF16)


def _subln(ot, bd_ref, g_ref, lam_init):
    o = ot.T
    y = o * lax.rsqrt(_group_mean(o * o, bd_ref[...]) + EPS) * g_ref[...]
    return (y * (1.0 - lam_init)).astype(BF16)


def _attn_ctx_body(q_ref, k_ref, v_ref, lam_ref, g_ref, bd_ref, o_ref, *, lam_init):
    q = q_ref[0]
    kall = k_ref[0]
    vt = v_ref[0].astype(F32).T.astype(BF16)
    lam = _lambda(lam_ref, lam_init)
    lane = lax.broadcasted_iota(jnp.int32, (1, GROUP_W), 1)
    map_of_lane = lax.shift_right_logical(lane, 5)
    outs = []
    for h in range(HEADS):
        s_maps = [lax.dot_general(kall, jnp.where(map_of_lane == 2 * h + m, q, jnp.zeros_like(q)),
                                  NT_DIMS, preferred_element_type=F32) for m in range(2)]
        at = _softmax_diff(s_maps, lam)
        outs.append(jnp.dot(vt[h * HEAD_DIM:(h + 1) * HEAD_DIM, :], at,
                            preferred_element_type=F32))
    o_ref[0] = _subln(jnp.concatenate(outs, axis=0), bd_ref, g_ref, lam_init)


def _attention_ctx(zc, lam_params, subln_row, bd, *, lam_init):
    B, L, _ = zc.shape
    return pl.pallas_call(
        functools.partial(_attn_ctx_body, lam_init=lam_init),
        grid=(B,),
        in_specs=[pl.BlockSpec((1, L, GROUP_W), lambda b: (b, 0, G_DQ)),
                  pl.BlockSpec((1, L, GROUP_W), lambda b: (b, 0, G_DK)),
                  pl.BlockSpec((1, L, GROUP_W), lambda b: (b, 0, G_DV)),
                  _const_spec((4, DIFF_QK)), _const_spec((1, GROUP_W)),
                  _const_spec((GROUP_W, GROUP_W))],
        out_specs=pl.BlockSpec((1, L, GROUP_W), lambda b: (b, 0, 0)),
        out_shape=jax.ShapeDtypeStruct((B, L, GROUP_W), BF16),
        compiler_params=_cparams(("parallel",)),
        name="attn_ctx",
    )(zc, zc, zc, lam_params, subln_row, bd)


def _attn_body(q_ref, kc_ref, vc_ref, k_ref, v_ref, lam_ref, g_ref, bd_ref, o_ref,
               s_s, vt_s, ot_s, mx_s, *, lam_init, tq):
    L = kc_ref.shape[1]
    S = k_ref.shape[1]
    n_tiles = S // tq
    vt = jnp.concatenate([vc_ref[0].astype(F32).T, v_ref[0].astype(F32).T], axis=1)
    for h in range(HEADS):
        vt_s[h, 0:HEAD_DIM, :] = vt[h * HEAD_DIM:(h + 1) * HEAD_DIM, :].astype(BF16)
        vt_s[h, HEAD_DIM:, :] = jnp.ones((ONES_ROWS, L + S), BF16)
    lam = _lambda(lam_ref, lam_init)
    lane = lax.broadcasted_iota(jnp.int32, (1, GROUP_W), 1)
    map_of_lane = lax.shift_right_logical(lane, 5)

    def tile_rows(i):
        r = i * tq
        return pl.ds(r if isinstance(r, int) else pl.multiple_of(r, tq), tq)

    def scores(i, h):
        q = q_ref[0, tile_rows(i), :]
        for m in range(2):
            qm = jnp.where(map_of_lane == 2 * h + m, q, jnp.zeros_like(q))
            sc = lax.dot_general(kc_ref[0], qm, NT_DIMS, preferred_element_type=F32)
            sl = lax.dot_general(k_ref[0], qm, NT_DIMS, preferred_element_type=F32)
            s_s[h, m, 0:L, :] = sc
            s_s[h, m, L:, :] = sl
            mx_s[h, m, 0:1, :] = jnp.maximum(jnp.max(sc, axis=0, keepdims=True),
                                             jnp.max(sl, axis=0, keepdims=True))

    def values(h):
        e1 = jnp.exp2(s_s[h, 0] - mx_s[h, 0, 0:1, :]).astype(BF16)
        e2 = jnp.exp2(s_s[h, 1] - mx_s[h, 1, 0:1, :]).astype(BF16)
        r1 = jnp.dot(vt_s[h], e1, preferred_element_type=F32)
        r2 = jnp.dot(vt_s[h], e2, preferred_element_type=F32)
        ot_s[h * HEAD_DIM:(h + 1) * HEAD_DIM, :] = (
            r1[0:HEAD_DIM] * (1.0 / r1[HEAD_DIM:HEAD_DIM + 1])
            - r2[0:HEAD_DIM] * (lam / r2[HEAD_DIM:HEAD_DIM + 1]))

    def finish(i):
        o_ref[0, tile_rows(i), :] = _subln(ot_s[...], bd_ref, g_ref, lam_init)

    def tile(i, first):
        scores(i, 0)
        if not first:
            values(3)
            finish(i - 1)
        scores(i, 1)
        values(0)
        scores(i, 2)
        values(1)
        scores(i, 3)
        values(2)

    tile(0, True)

    def loop_body(i, carry):
        tile(i, False)
        return carry

    lax.fori_loop(1, n_tiles, loop_body, 0)
    values(3)
    finish(n_tiles - 1)


def _attention(z, zc, lam_params, subln_row, bd, *, lam_init, tq):
    B, S, _ = z.shape
    L = zc.shape[1]
    nk = L + S
    return pl.pallas_call(
        functools.partial(_attn_body, lam_init=lam_init, tq=tq),
        grid=(B,),
        in_specs=[pl.BlockSpec((1, S, GROUP_W), lambda b: (b, 0, G_DQ)),
                  pl.BlockSpec((1, L, GROUP_W), lambda b: (b, 0, G_DK)),
                  pl.BlockSpec((1, L, GROUP_W), lambda b: (b, 0, G_DV)),
                  pl.BlockSpec((1, S, GROUP_W), lambda b: (b, 0, G_DK)),
                  pl.BlockSpec((1, S, GROUP_W), lambda b: (b, 0, G_DV)),
                  _const_spec((4, DIFF_QK)), _const_spec((1, GROUP_W)),
                  _const_spec((GROUP_W, GROUP_W))],
        out_specs=pl.BlockSpec((1, S, GROUP_W), lambda b: (b, 0, 0)),
        out_shape=jax.ShapeDtypeStruct((B, S, GROUP_W), BF16),
        scratch_shapes=[pltpu.VMEM((HEADS, 2, nk, tq), F32),
                        pltpu.VMEM((HEADS, V_ROWS, nk), BF16),
                        pltpu.VMEM((GROUP_W, tq), F32),
                        pltpu.VMEM((HEADS, 2, 8, tq), F32)],
        compiler_params=_cparams(("parallel",)),
        name="attn",
    )(z, zc, zc, z, z, lam_params, subln_row, bd)


def _tail_body(x_ref, xp_ref, xn_ref, ma_ref, map_ref, man_ref, md_ref, mdp_ref, mdn_ref,
               wo_ref, gt1_ref, g_ref, sh_ref, sc_ref, gt_ref, wu_ref, cw_ref, wd_ref,
               *rest, ts, final):
    if final:
        fg_ref, o_ref, x_s, m_s, h_s, a_s = rest
    else:
        o_ref, x_s, m_s, h_s, a_s = rest
    i = pl.program_id(1)
    last = pl.num_programs(1) - 1
    H = FFN_HALO
    k_abc = 3 * GROUP_W
    lo, mid, hi = slice(0, H), slice(H, H + ts), slice(H + ts, ts + 2 * H)
    for rows, xr, ar, dr in ((lo, xp_ref, map_ref, mdp_ref), (mid, x_ref, ma_ref, md_ref),
                             (hi, xn_ref, man_ref, mdn_ref)):
        x_s[rows, :] = xr[0]
        m_s[rows, 0:k_abc] = ar[0]
        m_s[rows, k_abc:] = dr[0]
    x1 = x_s[...] + gt1_ref[0] * jnp.dot(m_s[...], wo_ref[...], preferred_element_type=F32)
    x_s[...] = x1
    h = (_rms(x1) * g_ref[...]) * (1.0 + sc_ref[0]) + sh_ref[0]
    row = lax.broadcasted_iota(jnp.int32, (ts + 2 * H, 1), 0)
    inside = jnp.logical_and(jnp.logical_or(row >= H, i > 0),
                             jnp.logical_or(row < H + ts, i < last))
    h_s[...] = jnp.where(inside, h, 0.0).astype(BF16)

    acc = jnp.zeros((ts, D_MODEL), F32)
    for c0, c1 in FFN_BOUNDS:
        w = c1 - c0
        a_s[:, 0:w] = jnp.dot(h_s[...], wu_ref[:, c0:c1], preferred_element_type=F32)
        cw = cw_ref[:, c0:c1]
        conv = (a_s[pl.ds(H - 1, ts), 0:w] * cw[0:1]
                + a_s[pl.ds(H, ts), 0:w] * cw[1:2]
                + a_s[pl.ds(H + 1, ts), 0:w] * cw[2:3])
        b = jnp.dot(h_s[H:H + ts, :], wu_ref[:, D_FF + c0:D_FF + c1],
                    preferred_element_type=F32)
        hid = (jax.nn.silu(conv) * b).astype(BF16)
        acc = acc + jnp.dot(hid, wd_ref[c0:c1, :], preferred_element_type=F32)
    y = x_s[mid, :] + gt_ref[0] * acc
    if final:
        y = _rms(y) * fg_ref[...]
    o_ref[0] = y


def _tail(x, mabc, md, w_out_bf, g2, mod, w_up_bf, conv_w, w_down_bf, final_g, *, ctx, ts):
    B, S, D = x.shape
    brow = (lambda b: CTX_ROW) if ctx else (lambda b: b)
    H = FFN_HALO
    nh = S // H
    per = ts // H
    wmax = max(c1 - c0 for c0, c1 in FFN_BOUNDS)
    final = final_g is not None

    def tiled(width):
        return [
            pl.BlockSpec((1, ts, width), lambda b, i: (b, i, 0)),
            pl.BlockSpec((1, H, width), lambda b, i: (b, jnp.maximum(i * per - 1, 0), 0)),
            pl.BlockSpec((1, H, width), lambda b, i: (b, jnp.minimum((i + 1) * per, nh - 1), 0)),
        ]

    def mod_spec(col):
        return pl.BlockSpec((1, 1, D), lambda b, i: (brow(b), 0, col))

    in_specs = tiled(D) + tiled(3 * GROUP_W) + tiled(GROUP_W) + [
        _const_spec((D, D)),
        mod_spec(2),
        _const_spec((1, D)),
        mod_spec(3),
        mod_spec(4),
        mod_spec(5),
        _const_spec((D, 2 * D_FF)),
        _const_spec((3, D_FF)),
        _const_spec((D_FF, D)),
    ]
    args = [x, x, x, mabc, mabc, mabc, md, md, md, w_out_bf, mod, g2.reshape(1, D), mod, mod,
            mod, w_up_bf, conv_w, w_down_bf]
    if final:
        in_specs.append(_const_spec((1, D)))
        args.append(final_g.reshape(1, D))
    return pl.pallas_call(
        functools.partial(_tail_body, ts=ts, final=final),
        grid=(B, S // ts),
        in_specs=in_specs,
        out_specs=pl.BlockSpec((1, ts, D), lambda b, i: (b, i, 0)),
        out_shape=jax.ShapeDtypeStruct((B, S, D), F32),
        scratch_shapes=[pltpu.VMEM((ts + 2 * H, D), F32),
                        pltpu.VMEM((ts + 2 * H, D), BF16),
                        pltpu.VMEM((ts + 2 * H, D), BF16),
                        pltpu.VMEM((ts + 2 * H, wmax), F32)],
        compiler_params=_cparams(("parallel", "parallel")),
        name="tail_ctx" if ctx else "tail",
    )(*args)


def _rope_tables(n):
    pos = jnp.arange(n, dtype=jnp.int32)
    row = (pos // GRID_W).astype(F32)
    col = (pos % GRID_W).astype(F32)
    inv_freq = ROPE_THETA ** (-jnp.arange(0, ROT_HALF, 2, dtype=F32) / ROT_HALF)
    lane = np.arange(GROUP_W)
    axis = (lane // ROT_HALF) % 2
    half = (lane // (ROT_HALF // 2)) % 2
    freq = jnp.tile(inv_freq, GROUP_W // (ROT_HALF // 2))[None, :]
    ang = jnp.where(jnp.asarray(axis == 0)[None, :], row[:, None] * freq, col[:, None] * freq)
    cos = jnp.cos(ang)
    sin = jnp.sin(ang)
    sa = jnp.where(jnp.asarray(half == 0)[None, :], -sin, 0.0)
    sb = jnp.where(jnp.asarray(half == 1)[None, :], sin, 0.0)
    return cos, sa, sb


def _dft_tables(n):
    n2c = GRID_W
    n1c = n // n2c
    k = jnp.arange(n, dtype=jnp.int32)

    def trig(cols, stride):
        kn = (k[:, None] * (jnp.arange(cols, dtype=jnp.int32) * stride)[None, :]) % n
        ang = kn.astype(F32) * (2.0 * math.pi / n)
        return jnp.cos(ang), jnp.sin(ang)

    ca, sa = trig(n1c, n2c)
    cb, sb = trig(n2c, 1)
    s = 1.0 / math.sqrt(n)
    cos = ca[:, :, None] * cb[:, None, :] - sa[:, :, None] * sb[:, None, :]
    sin = sa[:, :, None] * cb[:, None, :] + ca[:, :, None] * sb[:, None, :]
    return ((cos * s).reshape(n, n).astype(BF16), (sin * (-s)).reshape(n, n).astype(BF16))


def _head_block_tables():
    eye = np.kron(np.eye(HEADS), np.ones((HEAD_DIM, HEAD_DIM)))
    k = np.arange(HEAD_DIM)
    ang = 2.0 * np.pi * ((k[:, None] * k[None, :]) % HEAD_DIM) / HEAD_DIM
    s = 1.0 / math.sqrt(HEAD_DIM)
    c64 = np.kron(np.eye(HEADS), np.cos(ang) * s)
    s64 = np.kron(np.eye(HEADS), np.sin(ang) * s)
    return tuple(jnp.asarray(t, F32).astype(BF16) for t in (eye / HEAD_DIM, c64, s64))


def kernel(x, c, ctx, c_ctx, w_ada, b_ada, norm1_g, norm2_g, w_in, gmlp_ws, gmlp_bs, sconv_w,
           lambda_q1, lambda_k1, lambda_q2, lambda_k2, subln_g, w_out, ffn_w_up, ffn_conv_w,
           ffn_w_down, final_g):
    B, N, D = x.shape
    L = ctx.shape[1]

    cc = jnp.zeros((ADA_ROWS, D), F32).at[:B].set(c).at[CTX_ROW].set(c_ctx)
    mod_all = _adaln(cc, w_ada, b_ada)

    rope_tabs = _rope_tables(N)
    bd, c64, s64 = _head_block_tables()
    dft_n = _dft_tables(N)
    dft_l = _dft_tables(L)

    xc = ctx
    for l in range(DEPTH):
        ctx_out = l < DEPTH - 1
        lam_init = 0.8 - 0.6 * math.exp(-0.3 * l)
        mod = mod_all[l].reshape(ADA_ROWS, 1, 6 * D)
        w_in_bf = w_in[l].astype(BF16)
        w_out_bf = w_out[l].astype(BF16)
        w_up_bf = ffn_w_up[l].astype(BF16)
        w_down_bf = ffn_w_down[l].astype(BF16)
        ws_rows = gmlp_ws[l].reshape(HEADS * CHUNK, CHUNK).astype(BF16)
        bias_full = jnp.repeat(gmlp_bs[l].T, HEAD_DIM, axis=1)
        lam_params = jnp.stack([lambda_q1[l], lambda_k1[l], lambda_q2[l], lambda_k2[l]])
        subln_row = jnp.tile(subln_g[l], HEADS).reshape(1, GROUP_W)
        fin = final_g if l == DEPTH - 1 else None

        z = _in_proj(x, norm1_g[l], mod, w_in_bf, rope_tabs, ctx=False, ts=1024)
        zc = _in_proj(xc, norm1_g[l], mod, w_in_bf, None, ctx=True, ts=L)

        mabc = _mixers(z, ws_rows, bias_full, sconv_w[l], bd, c64, s64, *dft_n, ctx=False)
        md = _attention(z, zc, lam_params, subln_row, bd, lam_init=lam_init, tq=256)
        x = _tail(x, mabc, md, w_out_bf, norm2_g[l], mod, w_up_bf, ffn_conv_w[l], w_down_bf,
                  fin, ctx=False, ts=512)

        if ctx_out:
            mabc_c = _mixers(zc, ws_rows, bias_full, sconv_w[l], bd, c64, s64, *dft_l, ctx=True)
            md_c = _attention_ctx(zc, lam_params, subln_row, bd, lam_init=lam_init)
            xc = _tail(xc, mabc_c, md_c, w_out_bf, norm2_g[l], mod, w_up_bf, ffn_conv_w[l],
                       w_down_bf, None, ctx=True, ts=L)
    return x
```

```python
import functools
import math

import numpy as np
import jax
import jax.numpy as jnp
from jax import lax
from jax.experimental import pallas as pl
from jax.experimental.pallas import tpu as pltpu

D_MODEL = 1024
DEPTH = 2
GRID_W = 64
N_GROUPS = 4
GROUP_W = D_MODEL // N_GROUPS
HEADS = 4
HEAD_DIM = GROUP_W // HEADS
CHUNK = 128
DIFF_QK = HEAD_DIM // 2
ROT_HALF = DIFF_QK // 2
ROPE_THETA = 10000.0
D_FF = 2816
EPS = 1e-6
G_AU, G_AV, G_BX, G_BB, G_BC, G_CF, G_DQ, G_DK, G_DV = range(9)
IN_COLS = 9 * GROUP_W

F32 = jnp.float32
BF16 = jnp.bfloat16

ADA_ROWS = 24
CTX_ROW = 16
FFN_HALO = 16
MXU_K = 256
FFN_BOUNDS = ((0, 6 * MXU_K), (6 * MXU_K, D_FF))
VMEM_LIMIT = 56 * 1024 * 1024
QK_SCALE = DIFF_QK ** -0.5 * math.log2(math.e)
NT_DIMS = (((1,), (1,)), ((), ()))
DFT_ROWS = 512
IN_SUB = 128
ONES_ROWS = 16
V_ROWS = HEAD_DIM + ONES_ROWS


def _cparams(sem):
    return pltpu.CompilerParams(dimension_semantics=sem, vmem_limit_bytes=VMEM_LIMIT)


def _const_spec(shape):
    nd = len(shape)
    return pl.BlockSpec(shape, lambda *_: (0,) * nd, pipeline_mode=pl.Buffered(1))


def _layer_spec(shape, l):
    nd = len(shape)
    return pl.BlockSpec((None,) + tuple(shape), lambda *_: (l,) + (0,) * nd,
                        pipeline_mode=pl.Buffered(1))


def _rms(x):
    return x * lax.rsqrt(jnp.mean(x * x, axis=-1, keepdims=True) + EPS)


def _group_mean(sq, bd):
    hi = sq.astype(BF16)
    lo = (sq - hi.astype(F32)).astype(BF16)
    return (jnp.dot(hi, bd, preferred_element_type=F32)
            + jnp.dot(lo, bd, preferred_element_type=F32))


def _ada_body(c_ref, w_ref, b_ref, o_ref):
    s = jax.nn.silu(c_ref[...]).astype(BF16)
    o_ref[0] = jnp.dot(s, w_ref[0].astype(BF16), preferred_element_type=F32) + b_ref[0]


def _adaln(cc, w_ada, b_ada):
    n_col = 6
    return pl.pallas_call(
        _ada_body,
        grid=(DEPTH, n_col),
        in_specs=[
            pl.BlockSpec((ADA_ROWS, D_MODEL), lambda l, j: (0, 0)),
            pl.BlockSpec((1, D_MODEL, D_MODEL), lambda l, j: (l, 0, j)),
            pl.BlockSpec((1, 1, D_MODEL), lambda l, j: (l, 0, j)),
        ],
        out_specs=pl.BlockSpec((1, ADA_ROWS, D_MODEL), lambda l, j: (l, 0, j)),
        out_shape=jax.ShapeDtypeStruct((DEPTH, ADA_ROWS, 6 * D_MODEL), F32),
        compiler_params=_cparams(("parallel", "parallel")),
        name="adaln",
    )(cc, w_ada, b_ada.reshape(DEPTH, 1, 6 * D_MODEL))


def _in_body(x_ref, g_ref, sh_ref, sc_ref, w_ref, *rest, rope):
    if rope:
        cos_ref, sa_ref, sb_ref, z_ref = rest
    else:
        (z_ref,) = rest
    ts = x_ref.shape[1]
    sub = min(ts, IN_SUB)
    q0 = G_DQ * GROUP_W
    for r in range(0, ts, sub):
        rows = slice(r, r + sub)
        h = _rms(x_ref[0, rows, :]) * g_ref[...]
        h = h * (1.0 + sc_ref[0]) + sh_ref[0]
        z = jnp.dot(h.astype(BF16), w_ref[...], preferred_element_type=F32)
        z_ref[0, rows, :q0] = z[:, :q0].astype(BF16)
        for g in (G_DQ, G_DK):
            t = z[:, g * GROUP_W:(g + 1) * GROUP_W]
            if rope:
                t = (t * cos_ref[rows, :]
                     + pltpu.roll(t, GROUP_W - ROT_HALF // 2, axis=1) * sa_ref[rows, :]
                     + pltpu.roll(t, ROT_HALF // 2, axis=1) * sb_ref[rows, :])
            if g == G_DQ:
                t = t * QK_SCALE
            z_ref[0, rows, g * GROUP_W:(g + 1) * GROUP_W] = t.astype(BF16)
        z_ref[0, rows, G_DV * GROUP_W:] = z[:, G_DV * GROUP_W:].astype(BF16)


def _in_kv_body(x_ref, g_ref, sh_ref, sc_ref, wk_ref, wv_ref, z_ref):
    h = _rms(x_ref[0]) * g_ref[...]
    h = (h * (1.0 + sc_ref[0]) + sh_ref[0]).astype(BF16)
    z_ref[0, :, 0:GROUP_W] = jnp.dot(h, wk_ref[...], preferred_element_type=F32).astype(BF16)
    z_ref[0, :, GROUP_W:] = jnp.dot(h, wv_ref[...], preferred_element_type=F32).astype(BF16)


def _in_proj_ctx_kv(xc, g1, mod, w_in_all, l):
    B, L, D = xc.shape

    def w_spec(grp):
        return pl.BlockSpec((None, D, GROUP_W), lambda b: (l, 0, grp),
                            pipeline_mode=pl.Buffered(1))

    return pl.pallas_call(
        _in_kv_body,
        grid=(B,),
        in_specs=[
            pl.BlockSpec((1, L, D), lambda b: (b, 0, 0)),
            _const_spec((1, D)),
            pl.BlockSpec((1, 1, D), lambda b: (CTX_ROW, 0, 0)),
            pl.BlockSpec((1, 1, D), lambda b: (CTX_ROW, 0, 1)),
            w_spec(G_DK),
            w_spec(G_DV),
        ],
        out_specs=pl.BlockSpec((1, L, 2 * GROUP_W), lambda b: (b, 0, 0)),
        out_shape=jax.ShapeDtypeStruct((B, L, 2 * GROUP_W), BF16),
        compiler_params=_cparams(("parallel",)),
        name="in_proj_ctx_kv",
    )(xc, g1.reshape(1, D), mod, mod, w_in_all, w_in_all)


def _in_proj(x, g1, mod, w_in_all, l, rope_tabs, *, ctx, ts):
    B, S, D = x.shape
    brow = (lambda b: CTX_ROW) if ctx else (lambda b: b)
    in_specs = [
        pl.BlockSpec((1, ts, D), lambda i, b: (b, i, 0)),
        _const_spec((1, D)),
        pl.BlockSpec((1, 1, D), lambda i, b: (brow(b), 0, 0)),
        pl.BlockSpec((1, 1, D), lambda i, b: (brow(b), 0, 1)),
        _layer_spec((D, IN_COLS), l),
    ]
    args = [x, g1.reshape(1, D), mod, mod, w_in_all]
    rope = rope_tabs is not None
    if rope:
        in_specs += [pl.BlockSpec((ts, GROUP_W), lambda i, b: (i, 0))] * 3
        args += list(rope_tabs)
    return pl.pallas_call(
        functools.partial(_in_body, rope=rope),
        grid=(S // ts, B),
        in_specs=in_specs,
        out_specs=pl.BlockSpec((1, ts, IN_COLS), lambda i, b: (b, i, 0)),
        out_shape=jax.ShapeDtypeStruct((B, S, IN_COLS), BF16),
        compiler_params=_cparams(("parallel", "parallel")),
        name="in_proj_ctx" if ctx else "in_proj",
    )(*args)


def _mix_body(z_ref, ws_ref, bias_ref, sc_ref, bd_ref, c64_ref, s64_ref, wc_ref, wsn_ref,
              o_ref, v_s, gc_s, gs_s, *, seq):
    lane = lax.broadcasted_iota(jnp.int32, (1, GROUP_W), 1)
    head_of_lane = lax.shift_right_logical(lane, 6)
    ws = ws_ref[...]
    bias = bias_ref[...]

    tile = min(seq, DFT_ROWS)
    n_t = seq // tile
    per = tile // CHUNK

    def zcols(rows, grp):
        return z_ref[0, rows, grp * GROUP_W:(grp + 1) * GROUP_W]

    def head_norm(rows):
        g = jax.nn.gelu(zcols(rows, G_AV).astype(F32))
        v_s[rows, :] = (g * lax.rsqrt(_group_mean(g * g, bd_ref[...]) + EPS)).astype(BF16)

    def gated_conv(t):
        rows = slice(t * tile, (t + 1) * tile)

        def cx_of(rs):
            return zcols(rs, G_BC).astype(F32) * zcols(rs, G_BX).astype(F32)

        cx = cx_of(rows)
        zero = jnp.zeros((1, GROUP_W), F32)
        before = cx_of(slice(t * tile - 16, t * tile))[15:16] if t > 0 else zero
        after = cx_of(slice((t + 1) * tile, (t + 1) * tile + 16))[0:1] if t < n_t - 1 else zero
        row = lax.broadcasted_iota(jnp.int32, (tile, 1), 0)
        prev = jnp.where(row == 0, before, pltpu.roll(cx, 1, axis=0))
        nxt = jnp.where(row == tile - 1, after, pltpu.roll(cx, tile - 1, axis=0))
        w = sc_ref[...]
        conv = prev * w[0:1] + cx * w[1:2] + nxt * w[2:3]
        o_ref[0, rows, GROUP_W:2 * GROUP_W] = (zcols(rows, G_BB).astype(F32) * conv).astype(BF16)

    def gate_chunk(c):
        r = c * CHUNK
        m4 = jnp.dot(ws, v_s[r:r + CHUNK, :], preferred_element_type=F32)
        mixed = m4[0:CHUNK]
        for h in range(1, HEADS):
            mixed = jnp.where(head_of_lane == h, m4[h * CHUNK:(h + 1) * CHUNK], mixed)
        u = jax.nn.gelu(z_ref[0, r:r + CHUNK, G_AU * GROUP_W:(G_AU + 1) * GROUP_W].astype(F32))
        o_ref[0, r:r + CHUNK, 0:GROUP_W] = (u * (mixed + bias)).astype(BF16)

    f = zcols(slice(None), G_CF)
    gc_s[...] = jnp.dot(f, c64_ref[...], preferred_element_type=F32).astype(BF16)
    gs_s[...] = jnp.dot(f, s64_ref[...], preferred_element_type=F32).astype(BF16)
    for t in range(n_t):
        rows = slice(t * tile, (t + 1) * tile)
        fo = (jnp.dot(wc_ref[rows, :], gc_s[...], preferred_element_type=F32)
              + jnp.dot(wsn_ref[rows, :], gs_s[...], preferred_element_type=F32))
        o_ref[0, rows, 2 * GROUP_W:3 * GROUP_W] = fo.astype(BF16)
        head_norm(rows)
        for c in range(t * per, (t + 1) * per):
            gate_chunk(c)
        gated_conv(t)


def _mixers(z, ws_rows, bias_full, sconv, bd, c64, s64, wc, wsn, *, ctx):
    B, S, _ = z.shape
    zc = (G_CF + 1) * GROUP_W
    return pl.pallas_call(
        functools.partial(_mix_body, seq=S),
        grid=(B,),
        in_specs=[
            pl.BlockSpec((1, S, zc), lambda b: (b, 0, 0)),
            _const_spec((HEADS * CHUNK, CHUNK)),
            _const_spec((CHUNK, GROUP_W)),
            _const_spec((3, GROUP_W)),
            _const_spec((GROUP_W, GROUP_W)),
            _const_spec((GROUP_W, GROUP_W)),
            _const_spec((GROUP_W, GROUP_W)),
            _const_spec((S, S)),
            _const_spec((S, S)),
        ],
        out_specs=pl.BlockSpec((1, S, 3 * GROUP_W), lambda b: (b, 0, 0)),
        out_shape=jax.ShapeDtypeStruct((B, S, 3 * GROUP_W), BF16),
        scratch_shapes=[pltpu.VMEM((S, GROUP_W), BF16)] * 3,
        compiler_params=_cparams(("parallel",)),
        name="mixers_ctx" if ctx else "mixers",
    )(z, ws_rows, bias_full, sconv, bd, c64, s64, wc, wsn)


def _lambda(lam_ref, lam_init):
    lp = lam_ref[...]
    return (jnp.exp(jnp.sum(lp[0:1] * lp[1:2], axis=-1, keepdims=True))
            - jnp.exp(jnp.sum(lp[2:3] * lp[3:4], axis=-1, keepdims=True)) + lam_init)


def _softmax_diff(s_maps, lam):
    es, cs = [], []
    for st in s_maps:
        e = jnp.exp2(st - jnp.max(st, axis=0, keepdims=True))
        es.append(e)
        cs.append(jnp.sum(e, axis=0, keepdims=True))
    return (es[0] * (1.0 / cs[0]) - es[1] * (lam / cs[1])).astype(BF16)


def _subln(ot, bd_ref, g_ref, lam_init):
    o = ot.T
    y = o * lax.rsqrt(_group_mean(o * o, bd_ref[...]) + EPS) * g_ref[...]
    return (y * (1.0 - lam_init)).astype(BF16)


def _attn_ctx_body(q_ref, k_ref, v_ref, lam_ref, g_ref, bd_ref, o_ref, *, lam_init):
    q = q_ref[0]
    kall = k_ref[0]
    vt = v_ref[0].astype(F32).T.astype(BF16)
    lam = _lambda(lam_ref, lam_init)
    lane = lax.broadcasted_iota(jnp.int32, (1, GROUP_W), 1)
    map_of_lane = lax.shift_right_logical(lane, 5)
    outs = []
    for h in range(HEADS):
        s_maps = [lax.dot_general(kall, jnp.where(map_of_lane == 2 * h + m, q, jnp.zeros_like(q)),
                                  NT_DIMS, preferred_element_type=F32) for m in range(2)]
        at = _softmax_diff(s_maps, lam)
        outs.append(jnp.dot(vt[h * HEAD_DIM:(h + 1) * HEAD_DIM, :], at,
                            preferred_element_type=F32))
    o_ref[0] = _subln(jnp.concatenate(outs, axis=0), bd_ref, g_ref, lam_init)


def _attention_ctx(zc, lam_params, subln_row, bd, *, lam_init):
    B, L, _ = zc.shape
    return pl.pallas_call(
        functools.partial(_attn_ctx_body, lam_init=lam_init),
        grid=(B,),
        in_specs=[pl.BlockSpec((1, L, GROUP_W), lambda b: (b, 0, G_DQ)),
                  pl.BlockSpec((1, L, GROUP_W), lambda b: (b, 0, G_DK)),
                  pl.BlockSpec((1, L, GROUP_W), lambda b: (b, 0, G_DV)),
                  _const_spec((4, DIFF_QK)), _const_spec((1, GROUP_W)),
                  _const_spec((GROUP_W, GROUP_W))],
        out_specs=pl.BlockSpec((1, L, GROUP_W), lambda b: (b, 0, 0)),
        out_shape=jax.ShapeDtypeStruct((B, L, GROUP_W), BF16),
        compiler_params=_cparams(("parallel",)),
        name="attn_ctx",
    )(zc, zc, zc, lam_params, subln_row, bd)


def _attn_body(q_ref, kc_ref, vc_ref, k_ref, v_ref, lam_ref, g_ref, bd_ref, o_ref,
               s_s, vt_s, ot_s, mx_s, *, lam_init, tq):
    L = kc_ref.shape[1]
    S = k_ref.shape[1]
    n_tiles = S // tq
    vt = jnp.concatenate([vc_ref[0].astype(F32).T, v_ref[0].astype(F32).T], axis=1)
    for h in range(HEADS):
        vt_s[h, 0:HEAD_DIM, :] = vt[h * HEAD_DIM:(h + 1) * HEAD_DIM, :].astype(BF16)
        vt_s[h, HEAD_DIM:, :] = jnp.ones((ONES_ROWS, L + S), BF16)
    lam = _lambda(lam_ref, lam_init)
    lane = lax.broadcasted_iota(jnp.int32, (1, GROUP_W), 1)
    map_of_lane = lax.shift_right_logical(lane, 5)

    def tile_rows(i):
        r = i * tq
        return pl.ds(r if isinstance(r, int) else pl.multiple_of(r, tq), tq)

    def scores(i, h):
        q = q_ref[0, tile_rows(i), :]
        for m in range(2):
            qm = jnp.where(map_of_lane == 2 * h + m, q, jnp.zeros_like(q))
            sc = lax.dot_general(kc_ref[0], qm, NT_DIMS, preferred_element_type=F32)
            sl = lax.dot_general(k_ref[0], qm, NT_DIMS, preferred_element_type=F32)
            s_s[h, m, 0:L, :] = sc
            s_s[h, m, L:, :] = sl
            mx_s[h, m, 0:1, :] = jnp.maximum(jnp.max(sc, axis=0, keepdims=True),
                                             jnp.max(sl, axis=0, keepdims=True))

    def values(h):
        e1 = jnp.exp2(s_s[h, 0] - mx_s[h, 0, 0:1, :]).astype(BF16)
        e2 = jnp.exp2(s_s[h, 1] - mx_s[h, 1, 0:1, :]).astype(BF16)
        r1 = jnp.dot(vt_s[h], e1, preferred_element_type=F32)
        r2 = jnp.dot(vt_s[h], e2, preferred_element_type=F32)
        ot_s[h * HEAD_DIM:(h + 1) * HEAD_DIM, :] = (
            r1[0:HEAD_DIM] * (1.0 / r1[HEAD_DIM:HEAD_DIM + 1])
            - r2[0:HEAD_DIM] * (lam / r2[HEAD_DIM:HEAD_DIM + 1]))

    def finish(i):
        o_ref[0, tile_rows(i), :] = _subln(ot_s[...], bd_ref, g_ref, lam_init)

    def tile(i, first):
        scores(i, 0)
        if not first:
            values(3)
            finish(i - 1)
        scores(i, 1)
        values(0)
        scores(i, 2)
        values(1)
        scores(i, 3)
        values(2)

    tile(0, True)

    def loop_body(i, carry):
        tile(i, False)
        return carry

    lax.fori_loop(1, n_tiles, loop_body, 0)
    values(3)
    finish(n_tiles - 1)


def _attention(z, zc, lam_params, subln_row, bd, *, lam_init, tq, ctx_kv_groups):
    B, S, _ = z.shape
    L = zc.shape[1]
    nk = L + S
    ck, cv = ctx_kv_groups
    return pl.pallas_call(
        functools.partial(_attn_body, lam_init=lam_init, tq=tq),
        grid=(B,),
        in_specs=[pl.BlockSpec((1, S, GROUP_W), lambda b: (b, 0, G_DQ)),
                  pl.BlockSpec((1, L, GROUP_W), lambda b: (b, 0, ck)),
                  pl.BlockSpec((1, L, GROUP_W), lambda b: (b, 0, cv)),
                  pl.BlockSpec((1, S, GROUP_W), lambda b: (b, 0, G_DK)),
                  pl.BlockSpec((1, S, GROUP_W), lambda b: (b, 0, G_DV)),
                  _const_spec((4, DIFF_QK)), _const_spec((1, GROUP_W)),
                  _const_spec((GROUP_W, GROUP_W))],
        out_specs=pl.BlockSpec((1, S, GROUP_W), lambda b: (b, 0, 0)),
        out_shape=jax.ShapeDtypeStruct((B, S, GROUP_W), BF16),
        scratch_shapes=[pltpu.VMEM((HEADS, 2, nk, tq), F32),
                        pltpu.VMEM((HEADS, V_ROWS, nk), BF16),
                        pltpu.VMEM((GROUP_W, tq), F32),
                        pltpu.VMEM((HEADS, 2, 8, tq), F32)],
        compiler_params=_cparams(("parallel",)),
        name="attn",
    )(z, zc, zc, z, z, lam_params, subln_row, bd)


def _tail_body(x_ref, xp_ref, xn_ref, ma_ref, map_ref, man_ref, md_ref, mdp_ref, mdn_ref,
               wo_ref, gt1_ref, g_ref, sh_ref, sc_ref, gt_ref, wu_ref, cw_ref, wd_ref,
               *rest, ts, final):
    if final:
        fg_ref, o_ref, x_s, m_s, h_s, a_s = rest
    else:
        o_ref, x_s, m_s, h_s, a_s = rest
    i = pl.program_id(1)
    last = pl.num_programs(1) - 1
    H = FFN_HALO
    k_abc = 3 * GROUP_W
    lo, mid, hi = slice(0, H), slice(H, H + ts), slice(H + ts, ts + 2 * H)
    for rows, xr, ar, dr in ((lo, xp_ref, map_ref, mdp_ref), (mid, x_ref, ma_ref, md_ref),
                             (hi, xn_ref, man_ref, mdn_ref)):
        x_s[rows, :] = xr[0]
        m_s[rows, 0:k_abc] = ar[0]
        m_s[rows, k_abc:] = dr[0]
    x1 = x_s[...] + gt1_ref[0] * jnp.dot(m_s[...], wo_ref[...], preferred_element_type=F32)
    x_s[...] = x1
    h = (_rms(x1) * g_ref[...]) * (1.0 + sc_ref[0]) + sh_ref[0]
    row = lax.broadcasted_iota(jnp.int32, (ts + 2 * H, 1), 0)
    inside = jnp.logical_and(jnp.logical_or(row >= H, i > 0),
                             jnp.logical_or(row < H + ts, i < last))
    h_s[...] = jnp.where(inside, h, 0.0).astype(BF16)

    acc = jnp.zeros((ts, D_MODEL), F32)
    for c0, c1 in FFN_BOUNDS:
        w = c1 - c0
        a_s[:, 0:w] = jnp.dot(h_s[...], wu_ref[:, c0:c1], preferred_element_type=F32)
        cw = cw_ref[:, c0:c1]
        conv = (a_s[pl.ds(H - 1, ts), 0:w] * cw[0:1]
                + a_s[pl.ds(H, ts), 0:w] * cw[1:2]
                + a_s[pl.ds(H + 1, ts), 0:w] * cw[2:3])
        b = jnp.dot(h_s[H:H + ts, :], wu_ref[:, D_FF + c0:D_FF + c1],
                    preferred_element_type=F32)
        hid = (jax.nn.silu(conv) * b).astype(BF16)
        acc = acc + jnp.dot(hid, wd_ref[c0:c1, :], preferred_element_type=F32)
    y = x_s[mid, :] + gt_ref[0] * acc
    if final:
        y = _rms(y) * fg_ref[...]
    o_ref[0] = y


def _tail(x, mabc, md, w_out_all, g2, mod, w_up_all, conv_w_all, w_down_all, final_g, l, *,
          ctx, ts):
    B, S, D = x.shape
    brow = (lambda b: CTX_ROW) if ctx else (lambda b: b)
    H = FFN_HALO
    nh = S // H
    per = ts // H
    wmax = max(c1 - c0 for c0, c1 in FFN_BOUNDS)
    final = final_g is not None

    def tiled(width):
        return [
            pl.BlockSpec((1, ts, width), lambda b, i: (b, i, 0)),
            pl.BlockSpec((1, H, width), lambda b, i: (b, jnp.maximum(i * per - 1, 0), 0)),
            pl.BlockSpec((1, H, width), lambda b, i: (b, jnp.minimum((i + 1) * per, nh - 1), 0)),
        ]

    def mod_spec(col):
        return pl.BlockSpec((1, 1, D), lambda b, i: (brow(b), 0, col))

    in_specs = tiled(D) + tiled(3 * GROUP_W) + tiled(GROUP_W) + [
        _layer_spec((D, D), l),
        mod_spec(2),
        _const_spec((1, D)),
        mod_spec(3),
        mod_spec(4),
        mod_spec(5),
        _layer_spec((D, 2 * D_FF), l),
        _layer_spec((3, D_FF), l),
        _layer_spec((D_FF, D), l),
    ]
    args = [x, x, x, mabc, mabc, mabc, md, md, md, w_out_all, mod, g2.reshape(1, D), mod, mod,
            mod, w_up_all, conv_w_all, w_down_all]
    if final:
        in_specs.append(_const_spec((1, D)))
        args.append(final_g.reshape(1, D))
    return pl.pallas_call(
        functools.partial(_tail_body, ts=ts, final=final),
        grid=(B, S // ts),
        in_specs=in_specs,
        out_specs=pl.BlockSpec((1, ts, D), lambda b, i: (b, i, 0)),
        out_shape=jax.ShapeDtypeStruct((B, S, D), F32),
        scratch_shapes=[pltpu.VMEM((ts + 2 * H, D), F32),
                        pltpu.VMEM((ts + 2 * H, D), BF16),
                        pltpu.VMEM((ts + 2 * H, D), BF16),
                        pltpu.VMEM((ts + 2 * H, wmax), F32)],
        compiler_params=_cparams(("parallel", "parallel")),
        name="tail_ctx" if ctx else "tail",
    )(*args)


def _rope_tables(n):
    pos = jnp.arange(n, dtype=jnp.int32)
    row = (pos // GRID_W).astype(F32)
    col = (pos % GRID_W).astype(F32)
    inv_freq = ROPE_THETA ** (-jnp.arange(0, ROT_HALF, 2, dtype=F32) / ROT_HALF)
    lane = np.arange(GROUP_W)
    axis = (lane // ROT_HALF) % 2
    half = (lane // (ROT_HALF // 2)) % 2
    freq = jnp.tile(inv_freq, GROUP_W // (ROT_HALF // 2))[None, :]
    ang = jnp.where(jnp.asarray(axis == 0)[None, :], row[:, None] * freq, col[:, None] * freq)
    cos = jnp.cos(ang)
    sin = jnp.sin(ang)
    sa = jnp.where(jnp.asarray(half == 0)[None, :], -sin, 0.0)
    sb = jnp.where(jnp.asarray(half == 1)[None, :], sin, 0.0)
    return cos, sa, sb


def _dft_tables(n):
    n2c = GRID_W
    n1c = n // n2c
    k = jnp.arange(n, dtype=jnp.int32)

    def trig(cols, stride):
        kn = (k[:, None] * (jnp.arange(cols, dtype=jnp.int32) * stride)[None, :]) % n
        ang = kn.astype(F32) * (2.0 * math.pi / n)
        return jnp.cos(ang), jnp.sin(ang)

    ca, sa = trig(n1c, n2c)
    cb, sb = trig(n2c, 1)
    s = 1.0 / math.sqrt(n)
    cos = ca[:, :, None] * cb[:, None, :] - sa[:, :, None] * sb[:, None, :]
    sin = sa[:, :, None] * cb[:, None, :] + ca[:, :, None] * sb[:, None, :]
    return ((cos * s).reshape(n, n).astype(BF16), (sin * (-s)).reshape(n, n).astype(BF16))


def _head_block_tables():
    eye = np.kron(np.eye(HEADS), np.ones((HEAD_DIM, HEAD_DIM)))
    k = np.arange(HEAD_DIM)
    ang = 2.0 * np.pi * ((k[:, None] * k[None, :]) % HEAD_DIM) / HEAD_DIM
    s = 1.0 / math.sqrt(HEAD_DIM)
    c64 = np.kron(np.eye(HEADS), np.cos(ang) * s)
    s64 = np.kron(np.eye(HEADS), np.sin(ang) * s)
    return tuple(jnp.asarray(t, F32).astype(BF16) for t in (eye / HEAD_DIM, c64, s64))


def kernel(x, c, ctx, c_ctx, w_ada, b_ada, norm1_g, norm2_g, w_in, gmlp_ws, gmlp_bs, sconv_w,
           lambda_q1, lambda_k1, lambda_q2, lambda_k2, subln_g, w_out, ffn_w_up, ffn_conv_w,
           ffn_w_down, final_g):
    B, N, D = x.shape
    L = ctx.shape[1]

    cc = jnp.zeros((ADA_ROWS, D), F32).at[:B].set(c).at[CTX_ROW].set(c_ctx)
    mod_all = _adaln(cc, w_ada, b_ada)

    rope_tabs = _rope_tables(N)
    bd, c64, s64 = _head_block_tables()
    dft_n = _dft_tables(N)
    dft_l = _dft_tables(L)

    w_in_all = w_in.astype(BF16)
    w_out_all = w_out.astype(BF16)
    w_up_all = ffn_w_up.astype(BF16)
    w_down_all = ffn_w_down.astype(BF16)

    xc = ctx
    for l in range(DEPTH):
        ctx_out = l < DEPTH - 1
        lam_init = 0.8 - 0.6 * math.exp(-0.3 * l)
        mod = mod_all[l].reshape(ADA_ROWS, 1, 6 * D)
        ws_rows = gmlp_ws[l].reshape(HEADS * CHUNK, CHUNK).astype(BF16)
        bias_full = jnp.repeat(gmlp_bs[l].T, HEAD_DIM, axis=1)
        lam_params = jnp.stack([lambda_q1[l], lambda_k1[l], lambda_q2[l], lambda_k2[l]])
        subln_row = jnp.tile(subln_g[l], HEADS).reshape(1, GROUP_W)
        fin = final_g if l == DEPTH - 1 else None

        z = _in_proj(x, norm1_g[l], mod, w_in_all, l, rope_tabs, ctx=False, ts=1024)
        if ctx_out:
            zc = _in_proj(xc, norm1_g[l], mod, w_in_all, l, None, ctx=True, ts=L)
            ctx_kv_groups = (G_DK, G_DV)
        else:
            zc = _in_proj_ctx_kv(xc, norm1_g[l], mod, w_in_all, l)
            ctx_kv_groups = (0, 1)

        mabc = _mixers(z, ws_rows, bias_full, sconv_w[l], bd, c64, s64, *dft_n, ctx=False)
        md = _attention(z, zc, lam_params, subln_row, bd, lam_init=lam_init, tq=256,
                        ctx_kv_groups=ctx_kv_groups)
        x = _tail(x, mabc, md, w_out_all, norm2_g[l], mod, w_up_all, ffn_conv_w, w_down_all,
                  fin, l, ctx=False, ts=512)

        if ctx_out:
            mabc_c = _mixers(zc, ws_rows, bias_full, sconv_w[l], bd, c64, s64, *dft_l, ctx=True)
            md_c = _attention_ctx(zc, lam_params, subln_row, bd, lam_init=lam_init)
            xc = _tail(xc, mabc_c, md_c, w_out_all, norm2_g[l], mod, w_up_all, ffn_conv_w,
                       w_down_all, None, l, ctx=True, ts=L)
    return x
```

```python
import functools
import math

import numpy as np
import jax
import jax.numpy as jnp
from jax import lax
from jax.experimental import pallas as pl
from jax.experimental.pallas import tpu as pltpu

D_MODEL = 1024
DEPTH = 2
GRID_W = 64
N_GROUPS = 4
GROUP_W = D_MODEL // N_GROUPS
HEADS = 4
HEAD_DIM = GROUP_W // HEADS
CHUNK = 128
DIFF_QK = HEAD_DIM // 2
ROT_HALF = DIFF_QK // 2
ROPE_THETA = 10000.0
D_FF = 2816
EPS = 1e-6
G_AU, G_AV, G_BX, G_BB, G_BC, G_CF, G_DQ, G_DK, G_DV = range(9)
IN_COLS = 9 * GROUP_W

F32 = jnp.float32
BF16 = jnp.bfloat16

ADA_ROWS = 24
CTX_ROW = 16
BF16_ROWS = 16
FFN_HALO = BF16_ROWS
ONES_ROWS = BF16_ROWS
V_ROWS = HEAD_DIM + ONES_ROWS
VMEM_LIMIT = 56 * 1024 * 1024
QK_SCALE = DIFF_QK ** -0.5 * math.log2(math.e)
NT_DIMS = (((1,), (1,)), ((), ()))

IN_ROWS = 1024
IN_SUB = 128
DFT_ROWS = 512
ATTN_QUERIES = 256
TAIL_ROWS = 512


def _cparams(sem):
    return pltpu.CompilerParams(dimension_semantics=sem, vmem_limit_bytes=VMEM_LIMIT)


def _const_spec(shape):
    nd = len(shape)
    return pl.BlockSpec(shape, lambda *_: (0,) * nd, pipeline_mode=pl.Buffered(1))


def _layer_spec(shape, l):
    nd = len(shape)
    return pl.BlockSpec((None,) + tuple(shape), lambda *_: (l,) + (0,) * nd,
                        pipeline_mode=pl.Buffered(1))


def _rms(x):
    return x * lax.rsqrt(jnp.mean(x * x, axis=-1, keepdims=True) + EPS)


def _group_mean(sq, bd):
    hi = sq.astype(BF16)
    lo = (sq - hi.astype(F32)).astype(BF16)
    return (jnp.dot(hi, bd, preferred_element_type=F32)
            + jnp.dot(lo, bd, preferred_element_type=F32))


def _ada_body(c_ref, w_ref, b_ref, o_ref):
    s = jax.nn.silu(c_ref[...]).astype(BF16)
    o_ref[0] = jnp.dot(s, w_ref[0].astype(BF16), preferred_element_type=F32) + b_ref[0]


def _adaln(cc, w_ada, b_ada):
    n_col = 6
    return pl.pallas_call(
        _ada_body,
        grid=(DEPTH, n_col),
        in_specs=[
            pl.BlockSpec((ADA_ROWS, D_MODEL), lambda l, j: (0, 0)),
            pl.BlockSpec((1, D_MODEL, D_MODEL), lambda l, j: (l, 0, j)),
            pl.BlockSpec((1, 1, D_MODEL), lambda l, j: (l, 0, j)),
        ],
        out_specs=pl.BlockSpec((1, ADA_ROWS, D_MODEL), lambda l, j: (l, 0, j)),
        out_shape=jax.ShapeDtypeStruct((DEPTH, ADA_ROWS, 6 * D_MODEL), F32),
        compiler_params=_cparams(("parallel", "parallel")),
        name="adaln",
    )(cc, w_ada, b_ada.reshape(DEPTH, 1, 6 * D_MODEL))


def _in_body(x_ref, g_ref, sh_ref, sc_ref, w_ref, *rest, rope):
    if rope:
        cos_ref, sa_ref, sb_ref, z_ref = rest
    else:
        (z_ref,) = rest
    ts = x_ref.shape[1]
    sub = min(ts, IN_SUB)
    q0 = G_DQ * GROUP_W
    for r in range(0, ts, sub):
        rows = slice(r, r + sub)
        h = _rms(x_ref[0, rows, :]) * g_ref[...]
        h = h * (1.0 + sc_ref[0]) + sh_ref[0]
        z = jnp.dot(h.astype(BF16), w_ref[...], preferred_element_type=F32)
        z_ref[0, rows, :q0] = z[:, :q0].astype(BF16)
        for g in (G_DQ, G_DK):
            t = z[:, g * GROUP_W:(g + 1) * GROUP_W]
            if rope:
                t = (t * cos_ref[rows, :]
                     + pltpu.roll(t, GROUP_W - ROT_HALF // 2, axis=1) * sa_ref[rows, :]
                     + pltpu.roll(t, ROT_HALF // 2, axis=1) * sb_ref[rows, :])
            if g == G_DQ:
                t = t * QK_SCALE
            z_ref[0, rows, g * GROUP_W:(g + 1) * GROUP_W] = t.astype(BF16)
        z_ref[0, rows, G_DV * GROUP_W:] = z[:, G_DV * GROUP_W:].astype(BF16)


def _in_kv_body(x_ref, g_ref, sh_ref, sc_ref, wk_ref, wv_ref, z_ref):
    h = _rms(x_ref[0]) * g_ref[...]
    h = (h * (1.0 + sc_ref[0]) + sh_ref[0]).astype(BF16)
    z_ref[0, :, 0:GROUP_W] = jnp.dot(h, wk_ref[...], preferred_element_type=F32).astype(BF16)
    z_ref[0, :, GROUP_W:] = jnp.dot(h, wv_ref[...], preferred_element_type=F32).astype(BF16)


def _in_proj_ctx_kv(xc, g1, mod, w_in_all, l):
    B, L, D = xc.shape

    def w_spec(grp):
        return pl.BlockSpec((None, D, GROUP_W), lambda b: (l, 0, grp),
                            pipeline_mode=pl.Buffered(1))

    return pl.pallas_call(
        _in_kv_body,
        grid=(B,),
        in_specs=[
            pl.BlockSpec((1, L, D), lambda b: (b, 0, 0)),
            _const_spec((1, D)),
            pl.BlockSpec((1, 1, D), lambda b: (CTX_ROW, 0, 0)),
            pl.BlockSpec((1, 1, D), lambda b: (CTX_ROW, 0, 1)),
            w_spec(G_DK),
            w_spec(G_DV),
        ],
        out_specs=pl.BlockSpec((1, L, 2 * GROUP_W), lambda b: (b, 0, 0)),
        out_shape=jax.ShapeDtypeStruct((B, L, 2 * GROUP_W), BF16),
        compiler_params=_cparams(("parallel",)),
        name="in_proj_ctx_kv",
    )(xc, g1.reshape(1, D), mod, mod, w_in_all, w_in_all)


def _in_proj(x, g1, mod, w_in_all, l, rope_tabs, *, ctx, ts):
    B, S, D = x.shape
    brow = (lambda b: CTX_ROW) if ctx else (lambda b: b)
    in_specs = [
        pl.BlockSpec((1, ts, D), lambda i, b: (b, i, 0)),
        _const_spec((1, D)),
        pl.BlockSpec((1, 1, D), lambda i, b: (brow(b), 0, 0)),
        pl.BlockSpec((1, 1, D), lambda i, b: (brow(b), 0, 1)),
        _layer_spec((D, IN_COLS), l),
    ]
    args = [x, g1.reshape(1, D), mod, mod, w_in_all]
    rope = rope_tabs is not None
    if rope:
        in_specs += [pl.BlockSpec((ts, GROUP_W), lambda i, b: (i, 0))] * 3
        args += list(rope_tabs)
    return pl.pallas_call(
        functools.partial(_in_body, rope=rope),
        grid=(S // ts, B),
        in_specs=in_specs,
        out_specs=pl.BlockSpec((1, ts, IN_COLS), lambda i, b: (b, i, 0)),
        out_shape=jax.ShapeDtypeStruct((B, S, IN_COLS), BF16),
        compiler_params=_cparams(("parallel", "parallel")),
        name="in_proj_ctx" if ctx else "in_proj",
    )(*args)


def _mix_body(z_ref, ws_ref, bias_ref, sc_ref, bd_ref, c64_ref, s64_ref, wc_ref, wsn_ref,
              o_ref, v_s, gc_s, gs_s, *, seq):
    lane = lax.broadcasted_iota(jnp.int32, (1, GROUP_W), 1)
    head_of_lane = lax.shift_right_logical(lane, 6)
    ws = ws_ref[...]
    bias = bias_ref[...]

    tile = min(seq, DFT_ROWS)
    n_t = seq // tile
    per = tile // CHUNK

    def zcols(rows, grp):
        return z_ref[0, rows, grp * GROUP_W:(grp + 1) * GROUP_W]

    def head_norm(rows):
        g = jax.nn.gelu(zcols(rows, G_AV).astype(F32))
        v_s[rows, :] = (g * lax.rsqrt(_group_mean(g * g, bd_ref[...]) + EPS)).astype(BF16)

    def gated_conv(t):
        rows = slice(t * tile, (t + 1) * tile)

        def cx_of(rs):
            return zcols(rs, G_BC).astype(F32) * zcols(rs, G_BX).astype(F32)

        cx = cx_of(rows)
        zero = jnp.zeros((1, GROUP_W), F32)
        nb = BF16_ROWS
        before = cx_of(slice(t * tile - nb, t * tile))[nb - 1:nb] if t > 0 else zero
        after = cx_of(slice((t + 1) * tile, (t + 1) * tile + nb))[0:1] if t < n_t - 1 else zero
        row = lax.broadcasted_iota(jnp.int32, (tile, 1), 0)
        prev = jnp.where(row == 0, before, pltpu.roll(cx, 1, axis=0))
        nxt = jnp.where(row == tile - 1, after, pltpu.roll(cx, tile - 1, axis=0))
        w = sc_ref[...]
        conv = prev * w[0:1] + cx * w[1:2] + nxt * w[2:3]
        o_ref[0, rows, GROUP_W:2 * GROUP_W] = (zcols(rows, G_BB).astype(F32) * conv).astype(BF16)

    def gate_chunk(c):
        r = c * CHUNK
        m4 = jnp.dot(ws, v_s[r:r + CHUNK, :], preferred_element_type=F32)
        mixed = m4[0:CHUNK]
        for h in range(1, HEADS):
            mixed = jnp.where(head_of_lane == h, m4[h * CHUNK:(h + 1) * CHUNK], mixed)
        u = jax.nn.gelu(z_ref[0, r:r + CHUNK, G_AU * GROUP_W:(G_AU + 1) * GROUP_W].astype(F32))
        o_ref[0, r:r + CHUNK, 0:GROUP_W] = (u * (mixed + bias)).astype(BF16)

    f = zcols(slice(None), G_CF)
    gc_s[...] = jnp.dot(f, c64_ref[...], preferred_element_type=F32).astype(BF16)
    gs_s[...] = jnp.dot(f, s64_ref[...], preferred_element_type=F32).astype(BF16)
    for t in range(n_t):
        rows = slice(t * tile, (t + 1) * tile)
        fo = (jnp.dot(wc_ref[rows, :], gc_s[...], preferred_element_type=F32)
              + jnp.dot(wsn_ref[rows, :], gs_s[...], preferred_element_type=F32))
        o_ref[0, rows, 2 * GROUP_W:3 * GROUP_W] = fo.astype(BF16)
        head_norm(rows)
        for c in range(t * per, (t + 1) * per):
            gate_chunk(c)
        gated_conv(t)


def _mixers(z, ws_rows, bias_full, sconv, bd, c64, s64, wc, wsn, *, ctx):
    B, S, _ = z.shape
    zc = (G_CF + 1) * GROUP_W
    return pl.pallas_call(
        functools.partial(_mix_body, seq=S),
        grid=(B,),
        in_specs=[
            pl.BlockSpec((1, S, zc), lambda b: (b, 0, 0)),
            _const_spec((HEADS * CHUNK, CHUNK)),
            _const_spec((CHUNK, GROUP_W)),
            _const_spec((3, GROUP_W)),
            _const_spec((GROUP_W, GROUP_W)),
            _const_spec((GROUP_W, GROUP_W)),
            _const_spec((GROUP_W, GROUP_W)),
            _const_spec((S, S)),
            _const_spec((S, S)),
        ],
        out_specs=pl.BlockSpec((1, S, 3 * GROUP_W), lambda b: (b, 0, 0)),
        out_shape=jax.ShapeDtypeStruct((B, S, 3 * GROUP_W), BF16),
        scratch_shapes=[pltpu.VMEM((S, GROUP_W), BF16)] * 3,
        compiler_params=_cparams(("parallel",)),
        name="mixers_ctx" if ctx else "mixers",
    )(z, ws_rows, bias_full, sconv, bd, c64, s64, wc, wsn)


def _lambda(lam_ref, lam_init):
    lp = lam_ref[...]
    return (jnp.exp(jnp.sum(lp[0:1] * lp[1:2], axis=-1, keepdims=True))
            - jnp.exp(jnp.sum(lp[2:3] * lp[3:4], axis=-1, keepdims=True)) + lam_init)


def _softmax_diff(s_maps, lam):
    es, cs = [], []
    for st in s_maps:
        e = jnp.exp2(st - jnp.max(st, axis=0, keepdims=True))
        es.append(e)
        cs.append(jnp.sum(e, axis=0, keepdims=True))
    return (es[0] * (1.0 / cs[0]) - es[1] * (lam / cs[1])).astype(BF16)


def _subln(ot, bd_ref, g_ref, lam_init):
    o = ot.T
    y = o * lax.rsqrt(_group_mean(o * o, bd_ref[...]) + EPS) * g_ref[...]
    return (y * (1.0 - lam_init)).astype(BF16)


def _attn_ctx_body(q_ref, k_ref, v_ref, lam_ref, g_ref, bd_ref, o_ref, *, lam_init):
    q = q_ref[0]
    kall = k_ref[0]
    vt = v_ref[0].astype(F32).T.astype(BF16)
    lam = _lambda(lam_ref, lam_init)
    lane = lax.broadcasted_iota(jnp.int32, (1, GROUP_W), 1)
    map_of_lane = lax.shift_right_logical(lane, 5)
    outs = []
    for h in range(HEADS):
        s_maps = [lax.dot_general(kall, jnp.where(map_of_lane == 2 * h + m, q, jnp.zeros_like(q)),
                                  NT_DIMS, preferred_element_type=F32) for m in range(2)]
        at = _softmax_diff(s_maps, lam)
        outs.append(jnp.dot(vt[h * HEAD_DIM:(h + 1) * HEAD_DIM, :], at,
                            preferred_element_type=F32))
    o_ref[0] = _subln(jnp.concatenate(outs, axis=0), bd_ref, g_ref, lam_init)


def _attention_ctx(zc, lam_params, subln_row, bd, *, lam_init):
    B, L, _ = zc.shape
    return pl.pallas_call(
        functools.partial(_attn_ctx_body, lam_init=lam_init),
        grid=(B,),
        in_specs=[pl.BlockSpec((1, L, GROUP_W), lambda b: (b, 0, G_DQ)),
                  pl.BlockSpec((1, L, GROUP_W), lambda b: (b, 0, G_DK)),
                  pl.BlockSpec((1, L, GROUP_W), lambda b: (b, 0, G_DV)),
                  _const_spec((4, DIFF_QK)), _const_spec((1, GROUP_W)),
                  _const_spec((GROUP_W, GROUP_W))],
        out_specs=pl.BlockSpec((1, L, GROUP_W), lambda b: (b, 0, 0)),
        out_shape=jax.ShapeDtypeStruct((B, L, GROUP_W), BF16),
        compiler_params=_cparams(("parallel",)),
        name="attn_ctx",
    )(zc, zc, zc, lam_params, subln_row, bd)


def _attn_body(q_ref, kc_ref, vc_ref, k_ref, v_ref, lam_ref, g_ref, bd_ref, o_ref,
               s_s, vt_s, ot_s, mx_s, *, lam_init, tq):
    L = kc_ref.shape[1]
    S = k_ref.shape[1]
    n_tiles = S // tq
    vt = jnp.concatenate([vc_ref[0].astype(F32).T, v_ref[0].astype(F32).T], axis=1)
    for h in range(HEADS):
        vt_s[h, 0:HEAD_DIM, :] = vt[h * HEAD_DIM:(h + 1) * HEAD_DIM, :].astype(BF16)
        vt_s[h, HEAD_DIM:, :] = jnp.ones((ONES_ROWS, L + S), BF16)
    lam = _lambda(lam_ref, lam_init)
    lane = lax.broadcasted_iota(jnp.int32, (1, GROUP_W), 1)
    map_of_lane = lax.shift_right_logical(lane, 5)

    def tile_rows(i):
        r = i * tq
        return pl.ds(r if isinstance(r, int) else pl.multiple_of(r, tq), tq)

    def scores(i, h):
        q = q_ref[0, tile_rows(i), :]
        for m in range(2):
            qm = jnp.where(map_of_lane == 2 * h + m, q, jnp.zeros_like(q))
            sc = lax.dot_general(kc_ref[0], qm, NT_DIMS, preferred_element_type=F32)
            sl = lax.dot_general(k_ref[0], qm, NT_DIMS, preferred_element_type=F32)
            s_s[h, m, 0:L, :] = sc
            s_s[h, m, L:, :] = sl
            mx_s[h, m, 0:1, :] = jnp.maximum(jnp.max(sc, axis=0, keepdims=True),
                                             jnp.max(sl, axis=0, keepdims=True))

    def values(h):
        e1 = jnp.exp2(s_s[h, 0] - mx_s[h, 0, 0:1, :]).astype(BF16)
        e2 = jnp.exp2(s_s[h, 1] - mx_s[h, 1, 0:1, :]).astype(BF16)
        r1 = jnp.dot(vt_s[h], e1, preferred_element_type=F32)
        r2 = jnp.dot(vt_s[h], e2, preferred_element_type=F32)
        ot_s[h * HEAD_DIM:(h + 1) * HEAD_DIM, :] = (
            r1[0:HEAD_DIM] * (1.0 / r1[HEAD_DIM:HEAD_DIM + 1])
            - r2[0:HEAD_DIM] * (lam / r2[HEAD_DIM:HEAD_DIM + 1]))

    def finish(i):
        o_ref[0, tile_rows(i), :] = _subln(ot_s[...], bd_ref, g_ref, lam_init)

    def tile(i, first):
        scores(i, 0)
        if not first:
            values(3)
            finish(i - 1)
        scores(i, 1)
        values(0)
        scores(i, 2)
        values(1)
        scores(i, 3)
        values(2)

    tile(0, True)

    def loop_body(i, carry):
        tile(i, False)
        return carry

    lax.fori_loop(1, n_tiles, loop_body, 0)
    values(3)
    finish(n_tiles - 1)


def _attention(z, zc, lam_params, subln_row, bd, *, lam_init, tq, ctx_kv_groups):
    B, S, _ = z.shape
    L = zc.shape[1]
    nk = L + S
    ck, cv = ctx_kv_groups
    return pl.pallas_call(
        functools.partial(_attn_body, lam_init=lam_init, tq=tq),
        grid=(B,),
        in_specs=[pl.BlockSpec((1, S, GROUP_W), lambda b: (b, 0, G_DQ)),
                  pl.BlockSpec((1, L, GROUP_W), lambda b: (b, 0, ck)),
                  pl.BlockSpec((1, L, GROUP_W), lambda b: (b, 0, cv)),
                  pl.BlockSpec((1, S, GROUP_W), lambda b: (b, 0, G_DK)),
                  pl.BlockSpec((1, S, GROUP_W), lambda b: (b, 0, G_DV)),
                  _const_spec((4, DIFF_QK)), _const_spec((1, GROUP_W)),
                  _const_spec((GROUP_W, GROUP_W))],
        out_specs=pl.BlockSpec((1, S, GROUP_W), lambda b: (b, 0, 0)),
        out_shape=jax.ShapeDtypeStruct((B, S, GROUP_W), BF16),
        scratch_shapes=[pltpu.VMEM((HEADS, 2, nk, tq), F32),
                        pltpu.VMEM((HEADS, V_ROWS, nk), BF16),
                        pltpu.VMEM((GROUP_W, tq), F32),
                        pltpu.VMEM((HEADS, 2, 8, tq), F32)],
        compiler_params=_cparams(("parallel",)),
        name="attn",
    )(z, zc, zc, z, z, lam_params, subln_row, bd)


def _tail_body(x_ref, xp_ref, xn_ref, ma_ref, map_ref, man_ref, md_ref, mdp_ref, mdn_ref,
               wo_ref, gt1_ref, g_ref, sh_ref, sc_ref, gt_ref, wu_ref, cw_ref, wd_ref,
               *rest, ts, final):
    if final:
        fg_ref, o_ref, x_s, m_s, h_s, a_s = rest
    else:
        o_ref, x_s, m_s, h_s, a_s = rest
    i = pl.program_id(1)
    last = pl.num_programs(1) - 1
    H = FFN_HALO
    k_abc = 3 * GROUP_W
    lo, mid, hi = slice(0, H), slice(H, H + ts), slice(H + ts, ts + 2 * H)
    for rows, xr, ar, dr in ((lo, xp_ref, map_ref, mdp_ref), (mid, x_ref, ma_ref, md_ref),
                             (hi, xn_ref, man_ref, mdn_ref)):
        x_s[rows, :] = xr[0]
        m_s[rows, 0:k_abc] = ar[0]
        m_s[rows, k_abc:] = dr[0]
    x1 = x_s[...] + gt1_ref[0] * jnp.dot(m_s[...], wo_ref[...], preferred_element_type=F32)
    x_s[...] = x1
    h = (_rms(x1) * g_ref[...]) * (1.0 + sc_ref[0]) + sh_ref[0]
    row = lax.broadcasted_iota(jnp.int32, (ts + 2 * H, 1), 0)
    inside = jnp.logical_and(jnp.logical_or(row >= H, i > 0),
                             jnp.logical_or(row < H + ts, i < last))
    h_s[...] = jnp.where(inside, h, 0.0).astype(BF16)

    a_s[...] = jnp.dot(h_s[...], wu_ref[:, 0:D_FF], preferred_element_type=F32)
    cw = cw_ref[...]
    conv = (a_s[pl.ds(H - 1, ts), :] * cw[0:1] + a_s[pl.ds(H, ts), :] * cw[1:2]
            + a_s[pl.ds(H + 1, ts), :] * cw[2:3])
    b = jnp.dot(h_s[mid, :], wu_ref[:, D_FF:], preferred_element_type=F32)
    hid = (jax.nn.silu(conv) * b).astype(BF16)
    y = x_s[mid, :] + gt_ref[0] * jnp.dot(hid, wd_ref[...], preferred_element_type=F32)
    if final:
        y = _rms(y) * fg_ref[...]
    o_ref[0] = y


def _tail(x, mabc, md, w_out_all, g2, mod, w_up_all, conv_w_all, w_down_all, final_g, l, *,
          ctx, ts):
    B, S, D = x.shape
    brow = (lambda b: CTX_ROW) if ctx else (lambda b: b)
    H = FFN_HALO
    nh = S // H
    per = ts // H
    final = final_g is not None

    def tiled(width):
        return [
            pl.BlockSpec((1, ts, width), lambda b, i: (b, i, 0)),
            pl.BlockSpec((1, H, width), lambda b, i: (b, jnp.maximum(i * per - 1, 0), 0)),
            pl.BlockSpec((1, H, width), lambda b, i: (b, jnp.minimum((i + 1) * per, nh - 1), 0)),
        ]

    def mod_spec(col):
        return pl.BlockSpec((1, 1, D), lambda b, i: (brow(b), 0, col))

    in_specs = tiled(D) + tiled(3 * GROUP_W) + tiled(GROUP_W) + [
        _layer_spec((D, D), l),
        mod_spec(2),
        _const_spec((1, D)),
        mod_spec(3),
        mod_spec(4),
        mod_spec(5),
        _layer_spec((D, 2 * D_FF), l),
        _layer_spec((3, D_FF), l),
        _layer_spec((D_FF, D), l),
    ]
    args = [x, x, x, mabc, mabc, mabc, md, md, md, w_out_all, mod, g2.reshape(1, D), mod, mod,
            mod, w_up_all, conv_w_all, w_down_all]
    if final:
        in_specs.append(_const_spec((1, D)))
        args.append(final_g.reshape(1, D))
    return pl.pallas_call(
        functools.partial(_tail_body, ts=ts, final=final),
        grid=(B, S // ts),
        in_specs=in_specs,
        out_specs=pl.BlockSpec((1, ts, D), lambda b, i: (b, i, 0)),
        out_shape=jax.ShapeDtypeStruct((B, S, D), F32),
        scratch_shapes=[pltpu.VMEM((ts + 2 * H, D), F32),
                        pltpu.VMEM((ts + 2 * H, D), BF16),
                        pltpu.VMEM((ts + 2 * H, D), BF16),
                        pltpu.VMEM((ts + 2 * H, D_FF), F32)],
        compiler_params=_cparams(("parallel", "parallel")),
        name="tail_ctx" if ctx else "tail",
    )(*args)


def _rope_tables(n):
    pos = jnp.arange(n, dtype=jnp.int32)
    row = (pos // GRID_W).astype(F32)
    col = (pos % GRID_W).astype(F32)
    inv_freq = ROPE_THETA ** (-jnp.arange(0, ROT_HALF, 2, dtype=F32) / ROT_HALF)
    lane = np.arange(GROUP_W)
    axis = (lane // ROT_HALF) % 2
    half = (lane // (ROT_HALF // 2)) % 2
    freq = jnp.tile(inv_freq, GROUP_W // (ROT_HALF // 2))[None, :]
    ang = jnp.where(jnp.asarray(axis == 0)[None, :], row[:, None] * freq, col[:, None] * freq)
    cos = jnp.cos(ang)
    sin = jnp.sin(ang)
    sa = jnp.where(jnp.asarray(half == 0)[None, :], -sin, 0.0)
    sb = jnp.where(jnp.asarray(half == 1)[None, :], sin, 0.0)
    return cos, sa, sb


def _dft_tables(n):
    n2c = GRID_W
    n1c = n // n2c
    k = jnp.arange(n, dtype=jnp.int32)

    def trig(cols, stride):
        kn = (k[:, None] * (jnp.arange(cols, dtype=jnp.int32) * stride)[None, :]) % n
        ang = kn.astype(F32) * (2.0 * math.pi / n)
        return jnp.cos(ang), jnp.sin(ang)

    ca, sa = trig(n1c, n2c)
    cb, sb = trig(n2c, 1)
    s = 1.0 / math.sqrt(n)
    cos = ca[:, :, None] * cb[:, None, :] - sa[:, :, None] * sb[:, None, :]
    sin = sa[:, :, None] * cb[:, None, :] + ca[:, :, None] * sb[:, None, :]
    return ((cos * s).astype(BF16).reshape(n, n), (sin * (-s)).astype(BF16).reshape(n, n))


def _head_block_tables():
    eye = np.kron(np.eye(HEADS), np.ones((HEAD_DIM, HEAD_DIM)))
    k = np.arange(HEAD_DIM)
    ang = 2.0 * np.pi * ((k[:, None] * k[None, :]) % HEAD_DIM) / HEAD_DIM
    s = 1.0 / math.sqrt(HEAD_DIM)
    c64 = np.kron(np.eye(HEADS), np.cos(ang) * s)
    s64 = np.kron(np.eye(HEADS), np.sin(ang) * s)
    return tuple(jnp.asarray(t, F32).astype(BF16) for t in (eye / HEAD_DIM, c64, s64))


def kernel(x, c, ctx, c_ctx, w_ada, b_ada, norm1_g, norm2_g, w_in, gmlp_ws, gmlp_bs, sconv_w,
           lambda_q1, lambda_k1, lambda_q2, lambda_k2, subln_g, w_out, ffn_w_up, ffn_conv_w,
           ffn_w_down, final_g):
    B, N, D = x.shape
    L = ctx.shape[1]

    cc = jnp.zeros((ADA_ROWS, D), F32).at[:B].set(c).at[CTX_ROW].set(c_ctx)
    mod_all = _adaln(cc, w_ada, b_ada)

    rope_tabs = _rope_tables(N)
    bd, c64, s64 = _head_block_tables()
    dft_n = _dft_tables(N)
    dft_l = _dft_tables(L)

    w_in_all = w_in.astype(BF16)
    w_out_all = w_out.astype(BF16)
    w_up_all = ffn_w_up.astype(BF16)
    w_down_all = ffn_w_down.astype(BF16)

    xc = ctx
    for l in range(DEPTH):
        ctx_out = l < DEPTH - 1
        lam_init = 0.8 - 0.6 * math.exp(-0.3 * l)
        mod = mod_all[l].reshape(ADA_ROWS, 1, 6 * D)
        ws_rows = gmlp_ws[l].reshape(HEADS * CHUNK, CHUNK).astype(BF16)
        bias_full = jnp.repeat(gmlp_bs[l].T, HEAD_DIM, axis=1)
        lam_params = jnp.stack([lambda_q1[l], lambda_k1[l], lambda_q2[l], lambda_k2[l]])
        subln_row = jnp.tile(subln_g[l], HEADS).reshape(1, GROUP_W)
        fin = final_g if l == DEPTH - 1 else None

        z = _in_proj(x, norm1_g[l], mod, w_in_all, l, rope_tabs, ctx=False, ts=IN_ROWS)
        if ctx_out:
            zc = _in_proj(xc, norm1_g[l], mod, w_in_all, l, None, ctx=True, ts=L)
            ctx_kv_groups = (G_DK, G_DV)
        else:
            zc = _in_proj_ctx_kv(xc, norm1_g[l], mod, w_in_all, l)
            ctx_kv_groups = (0, 1)

        mabc = _mixers(z, ws_rows, bias_full, sconv_w[l], bd, c64, s64, *dft_n, ctx=False)
        md = _attention(z, zc, lam_params, subln_row, bd, lam_init=lam_init, tq=ATTN_QUERIES,
                        ctx_kv_groups=ctx_kv_groups)
        x = _tail(x, mabc, md, w_out_all, norm2_g[l], mod, w_up_all, ffn_conv_w, w_down_all,
                  fin, l, ctx=False, ts=TAIL_ROWS)

        if ctx_out:
            mabc_c = _mixers(zc, ws_rows, bias_full, sconv_w[l], bd, c64, s64, *dft_l, ctx=True)
            md_c = _attention_ctx(zc, lam_params, subln_row, bd, lam_init=lam_init)
            xc = _tail(xc, mabc_c, md_c, w_out_all, norm2_g[l], mod, w_up_all, ffn_conv_w,
                       w_down_all, None, l, ctx=True, ts=L)
    return x
```

```python
import functools
import math

import numpy as np
import jax
import jax.numpy as jnp
from jax import lax
from jax.experimental import pallas as pl
from jax.experimental.pallas import tpu as pltpu

D_MODEL = 1024
DEPTH = 2
GRID_W = 64
N_GROUPS = 4
GROUP_W = D_MODEL // N_GROUPS
HEADS = 4
HEAD_DIM = GROUP_W // HEADS
CHUNK = 128
DIFF_QK = HEAD_DIM // 2
ROT_HALF = DIFF_QK // 2
ROPE_THETA = 10000.0
D_FF = 2816
EPS = 1e-6
G_AU, G_AV, G_BX, G_BB, G_BC, G_CF, G_DQ, G_DK, G_DV = range(9)
IN_COLS = 9 * GROUP_W

F32 = jnp.float32
BF16 = jnp.bfloat16

ADA_ROWS = 24
CTX_ROW = 16
BF16_ROWS = 16
FFN_HALO = BF16_ROWS
ONES_ROWS = BF16_ROWS
V_ROWS = HEAD_DIM + ONES_ROWS
VMEM_LIMIT = 56 * 1024 * 1024
QK_SCALE = DIFF_QK ** -0.5 * math.log2(math.e)
NT_DIMS = (((1,), (1,)), ((), ()))

IN_ROWS = 2048
IN_SUB = 128
CTX_IN_ROWS = 1024
DFT_ROWS = 256
ATTN_QUERIES = 256
TAIL_ROWS = 512


def _cparams(sem):
    return pltpu.CompilerParams(dimension_semantics=sem, vmem_limit_bytes=VMEM_LIMIT)


def _const_spec(shape):
    nd = len(shape)
    return pl.BlockSpec(shape, lambda *_: (0,) * nd, pipeline_mode=pl.Buffered(1))


def _layer_spec(shape, l):
    nd = len(shape)
    return pl.BlockSpec((None,) + tuple(shape), lambda *_: (l,) + (0,) * nd,
                        pipeline_mode=pl.Buffered(1))


def _rms(x):
    return x * lax.rsqrt(jnp.mean(x * x, axis=-1, keepdims=True) + EPS)


def _group_mean(sq, bd):
    hi = sq.astype(BF16)
    lo = (sq - hi.astype(F32)).astype(BF16)
    return (jnp.dot(hi, bd, preferred_element_type=F32)
            + jnp.dot(lo, bd, preferred_element_type=F32))


def _ada_body(c_ref, w_ref, b_ref, o_ref):
    s = jax.nn.silu(c_ref[...]).astype(BF16)
    o_ref[0] = jnp.dot(s, w_ref[0].astype(BF16), preferred_element_type=F32) + b_ref[0]


def _adaln(cc, w_ada, b_ada):
    n_col = 6
    return pl.pallas_call(
        _ada_body,
        grid=(DEPTH, n_col),
        in_specs=[
            pl.BlockSpec((ADA_ROWS, D_MODEL), lambda l, j: (0, 0)),
            pl.BlockSpec((1, D_MODEL, D_MODEL), lambda l, j: (l, 0, j)),
            pl.BlockSpec((1, 1, D_MODEL), lambda l, j: (l, 0, j)),
        ],
        out_specs=pl.BlockSpec((1, ADA_ROWS, D_MODEL), lambda l, j: (l, 0, j)),
        out_shape=jax.ShapeDtypeStruct((DEPTH, ADA_ROWS, 6 * D_MODEL), F32),
        compiler_params=_cparams(("parallel", "parallel")),
        name="adaln",
    )(cc, w_ada, b_ada.reshape(DEPTH, 1, 6 * D_MODEL))


def _in_body(x_ref, g_ref, sh_ref, sc_ref, w_ref, *rest, rope):
    if rope:
        cos_ref, sa_ref, sb_ref, z_ref = rest
    else:
        (z_ref,) = rest
    ts = x_ref.shape[1]
    sub = min(ts, IN_SUB)
    q0 = G_DQ * GROUP_W
    for r in range(0, ts, sub):
        rows = slice(r, r + sub)
        h = _rms(x_ref[0, rows, :]) * g_ref[...]
        h = h * (1.0 + sc_ref[0]) + sh_ref[0]
        z = jnp.dot(h.astype(BF16), w_ref[...], preferred_element_type=F32)
        z_ref[0, rows, :q0] = z[:, :q0].astype(BF16)
        for g in (G_DQ, G_DK):
            t = z[:, g * GROUP_W:(g + 1) * GROUP_W]
            if rope:
                t = (t * cos_ref[rows, :]
                     + pltpu.roll(t, GROUP_W - ROT_HALF // 2, axis=1) * sa_ref[rows, :]
                     + pltpu.roll(t, ROT_HALF // 2, axis=1) * sb_ref[rows, :])
            if g == G_DQ:
                t = t * QK_SCALE
            z_ref[0, rows, g * GROUP_W:(g + 1) * GROUP_W] = t.astype(BF16)
        z_ref[0, rows, G_DV * GROUP_W:] = z[:, G_DV * GROUP_W:].astype(BF16)


def _in_kv_body(x_ref, g_ref, sh_ref, sc_ref, wk_ref, wv_ref, z_ref):
    h = _rms(x_ref[0]) * g_ref[...]
    h = (h * (1.0 + sc_ref[0]) + sh_ref[0]).astype(BF16)
    z_ref[0, :, 0:GROUP_W] = jnp.dot(h, wk_ref[...], preferred_element_type=F32).astype(BF16)
    z_ref[0, :, GROUP_W:] = jnp.dot(h, wv_ref[...], preferred_element_type=F32).astype(BF16)


def _in_proj_ctx_kv(xc, g1, mod, w_in_all, l, *, ts):
    B, S, D = xc.shape

    def w_spec(grp):
        return pl.BlockSpec((None, D, GROUP_W), lambda b, i: (l, 0, grp),
                            pipeline_mode=pl.Buffered(1))

    return pl.pallas_call(
        _in_kv_body,
        grid=(B, S // ts),
        in_specs=[
            pl.BlockSpec((1, ts, D), lambda b, i: (b, i, 0)),
            _const_spec((1, D)),
            pl.BlockSpec((1, 1, D), lambda b, i: (CTX_ROW, 0, 0)),
            pl.BlockSpec((1, 1, D), lambda b, i: (CTX_ROW, 0, 1)),
            w_spec(G_DK),
            w_spec(G_DV),
        ],
        out_specs=pl.BlockSpec((1, ts, 2 * GROUP_W), lambda b, i: (b, i, 0)),
        out_shape=jax.ShapeDtypeStruct((B, S, 2 * GROUP_W), BF16),
        compiler_params=_cparams(("parallel", "parallel")),
        name="in_proj_ctx_kv",
    )(xc, g1.reshape(1, D), mod, mod, w_in_all, w_in_all)


def _in_proj(x, g1, mod, w_in_all, l, rope_tabs, *, ctx, ts):
    B, S, D = x.shape
    brow = (lambda b: CTX_ROW) if ctx else (lambda b: b)
    in_specs = [
        pl.BlockSpec((1, ts, D), lambda i, b: (b, i, 0)),
        _const_spec((1, D)),
        pl.BlockSpec((1, 1, D), lambda i, b: (brow(b), 0, 0)),
        pl.BlockSpec((1, 1, D), lambda i, b: (brow(b), 0, 1)),
        _layer_spec((D, IN_COLS), l),
    ]
    args = [x, g1.reshape(1, D), mod, mod, w_in_all]
    rope = rope_tabs is not None
    if rope:
        in_specs += [pl.BlockSpec((ts, GROUP_W), lambda i, b: (i, 0))] * 3
        args += list(rope_tabs)
    return pl.pallas_call(
        functools.partial(_in_body, rope=rope),
        grid=(S // ts, B),
        in_specs=in_specs,
        out_specs=pl.BlockSpec((1, ts, IN_COLS), lambda i, b: (b, i, 0)),
        out_shape=jax.ShapeDtypeStruct((B, S, IN_COLS), BF16),
        compiler_params=_cparams(("parallel", "parallel")),
        name="in_proj_ctx" if ctx else "in_proj",
    )(*args)


def _mix_body(z_ref, ws_ref, bias_ref, sc_ref, bd_ref, c64_ref, s64_ref, wc_ref, wsn_ref,
              o_ref, v_s, gc_s, gs_s, *, seq):
    lane = lax.broadcasted_iota(jnp.int32, (1, GROUP_W), 1)
    head_of_lane = lax.shift_right_logical(lane, 6)
    ws = ws_ref[...]
    bias = bias_ref[...]

    tile = min(seq, DFT_ROWS)
    n_t = seq // tile
    per = tile // CHUNK

    def zcols(rows, grp):
        return z_ref[0, rows, grp * GROUP_W:(grp + 1) * GROUP_W]

    def head_norm(rows):
        g = jax.nn.gelu(zcols(rows, G_AV).astype(F32))
        v_s[rows, :] = (g * lax.rsqrt(_group_mean(g * g, bd_ref[...]) + EPS)).astype(BF16)

    def gated_conv(t):
        rows = slice(t * tile, (t + 1) * tile)

        def cx_of(rs):
            return zcols(rs, G_BC).astype(F32) * zcols(rs, G_BX).astype(F32)

        cx = cx_of(rows)
        zero = jnp.zeros((1, GROUP_W), F32)
        nb = BF16_ROWS
        before = cx_of(slice(t * tile - nb, t * tile))[nb - 1:nb] if t > 0 else zero
        after = cx_of(slice((t + 1) * tile, (t + 1) * tile + nb))[0:1] if t < n_t - 1 else zero
        row = lax.broadcasted_iota(jnp.int32, (tile, 1), 0)
        prev = jnp.where(row == 0, before, pltpu.roll(cx, 1, axis=0))
        nxt = jnp.where(row == tile - 1, after, pltpu.roll(cx, tile - 1, axis=0))
        w = sc_ref[...]
        conv = prev * w[0:1] + cx * w[1:2] + nxt * w[2:3]
        o_ref[0, rows, GROUP_W:2 * GROUP_W] = (zcols(rows, G_BB).astype(F32) * conv).astype(BF16)

    def gate_chunk(c):
        r = c * CHUNK
        m4 = jnp.dot(ws, v_s[r:r + CHUNK, :], preferred_element_type=F32)
        mixed = m4[0:CHUNK]
        for h in range(1, HEADS):
            mixed = jnp.where(head_of_lane == h, m4[h * CHUNK:(h + 1) * CHUNK], mixed)
        u = jax.nn.gelu(z_ref[0, r:r + CHUNK, G_AU * GROUP_W:(G_AU + 1) * GROUP_W].astype(F32))
        o_ref[0, r:r + CHUNK, 0:GROUP_W] = (u * (mixed + bias)).astype(BF16)

    f = zcols(slice(None), G_CF)
    gc_s[...] = jnp.dot(f, c64_ref[...], preferred_element_type=F32).astype(BF16)
    gs_s[...] = jnp.dot(f, s64_ref[...], preferred_element_type=F32).astype(BF16)
    for t in range(n_t):
        rows = slice(t * tile, (t + 1) * tile)
        fo = (jnp.dot(wc_ref[rows, :], gc_s[...], preferred_element_type=F32)
              + jnp.dot(wsn_ref[rows, :], gs_s[...], preferred_element_type=F32))
        o_ref[0, rows, 2 * GROUP_W:3 * GROUP_W] = fo.astype(BF16)
        head_norm(rows)
        for c in range(t * per, (t + 1) * per):
            gate_chunk(c)
        gated_conv(t)


def _mixers(z, ws_rows, bias_full, sconv, bd, c64, s64, wc, wsn, *, ctx):
    B, S, _ = z.shape
    zc = (G_CF + 1) * GROUP_W
    return pl.pallas_call(
        functools.partial(_mix_body, seq=S),
        grid=(B,),
        in_specs=[
            pl.BlockSpec((1, S, zc), lambda b: (b, 0, 0)),
            _const_spec((HEADS * CHUNK, CHUNK)),
            _const_spec((CHUNK, GROUP_W)),
            _const_spec((3, GROUP_W)),
            _const_spec((GROUP_W, GROUP_W)),
            _const_spec((GROUP_W, GROUP_W)),
            _const_spec((GROUP_W, GROUP_W)),
            _const_spec((S, S)),
            _const_spec((S, S)),
        ],
        out_specs=pl.BlockSpec((1, S, 3 * GROUP_W), lambda b: (b, 0, 0)),
        out_shape=jax.ShapeDtypeStruct((B, S, 3 * GROUP_W), BF16),
        scratch_shapes=[pltpu.VMEM((S, GROUP_W), BF16)] * 3,
        compiler_params=_cparams(("parallel",)),
        name="mixers_ctx" if ctx else "mixers",
    )(z, ws_rows, bias_full, sconv, bd, c64, s64, wc, wsn)


def _lambda(lam_ref, lam_init):
    lp = lam_ref[...]
    return (jnp.exp(jnp.sum(lp[0:1] * lp[1:2], axis=-1, keepdims=True))
            - jnp.exp(jnp.sum(lp[2:3] * lp[3:4], axis=-1, keepdims=True)) + lam_init)


def _softmax_diff(s_maps, lam):
    es, cs = [], []
    for st in s_maps:
        e = jnp.exp2(st - jnp.max(st, axis=0, keepdims=True))
        es.append(e)
        cs.append(jnp.sum(e, axis=0, keepdims=True))
    return (es[0] * (1.0 / cs[0]) - es[1] * (lam / cs[1])).astype(BF16)


def _subln(ot, bd_ref, g_ref, lam_init):
    o = ot.T
    y = o * lax.rsqrt(_group_mean(o * o, bd_ref[...]) + EPS) * g_ref[...]
    return (y * (1.0 - lam_init)).astype(BF16)


def _attn_ctx_body(q_ref, k_ref, v_ref, lam_ref, g_ref, bd_ref, o_ref, *, lam_init):
    q = q_ref[0]
    kall = k_ref[0]
    vt = v_ref[0].astype(F32).T.astype(BF16)
    lam = _lambda(lam_ref, lam_init)
    lane = lax.broadcasted_iota(jnp.int32, (1, GROUP_W), 1)
    map_of_lane = lax.shift_right_logical(lane, 5)
    outs = []
    for h in range(HEADS):
        s_maps = [lax.dot_general(kall, jnp.where(map_of_lane == 2 * h + m, q, jnp.zeros_like(q)),
                                  NT_DIMS, preferred_element_type=F32) for m in range(2)]
        at = _softmax_diff(s_maps, lam)
        outs.append(jnp.dot(vt[h * HEAD_DIM:(h + 1) * HEAD_DIM, :], at,
                            preferred_element_type=F32))
    o_ref[0] = _subln(jnp.concatenate(outs, axis=0), bd_ref, g_ref, lam_init)


def _attention_ctx(zc, lam_params, subln_row, bd, *, lam_init):
    B, L, _ = zc.shape
    return pl.pallas_call(
        functools.partial(_attn_ctx_body, lam_init=lam_init),
        grid=(B,),
        in_specs=[pl.BlockSpec((1, L, GROUP_W), lambda b: (b, 0, G_DQ)),
                  pl.BlockSpec((1, L, GROUP_W), lambda b: (b, 0, G_DK)),
                  pl.BlockSpec((1, L, GROUP_W), lambda b: (b, 0, G_DV)),
                  _const_spec((4, DIFF_QK)), _const_spec((1, GROUP_W)),
                  _const_spec((GROUP_W, GROUP_W))],
        out_specs=pl.BlockSpec((1, L, GROUP_W), lambda b: (b, 0, 0)),
        out_shape=jax.ShapeDtypeStruct((B, L, GROUP_W), BF16),
        compiler_params=_cparams(("parallel",)),
        name="attn_ctx",
    )(zc, zc, zc, lam_params, subln_row, bd)


def _attn_body(q_ref, kc_ref, vc_ref, k_ref, v_ref, lam_ref, g_ref, bd_ref, o_ref,
               s_s, vt_s, ot_s, mx_s, *, lam_init, tq):
    L = kc_ref.shape[1]
    S = k_ref.shape[1]
    n_tiles = S // tq
    vt = jnp.concatenate([vc_ref[0].astype(F32).T, v_ref[0].astype(F32).T], axis=1)
    for h in range(HEADS):
        vt_s[h, 0:HEAD_DIM, :] = vt[h * HEAD_DIM:(h + 1) * HEAD_DIM, :].astype(BF16)
        vt_s[h, HEAD_DIM:, :] = jnp.ones((ONES_ROWS, L + S), BF16)
    lam = _lambda(lam_ref, lam_init)
    lane = lax.broadcasted_iota(jnp.int32, (1, GROUP_W), 1)
    map_of_lane = lax.shift_right_logical(lane, 5)

    def tile_rows(i):
        r = i * tq
        return pl.ds(r if isinstance(r, int) else pl.multiple_of(r, tq), tq)

    def scores(i, h):
        q = q_ref[0, tile_rows(i), :]
        for m in range(2):
            qm = jnp.where(map_of_lane == 2 * h + m, q, jnp.zeros_like(q))
            sc = lax.dot_general(kc_ref[0], qm, NT_DIMS, preferred_element_type=F32)
            sl = lax.dot_general(k_ref[0], qm, NT_DIMS, preferred_element_type=F32)
            s_s[h, m, 0:L, :] = sc
            s_s[h, m, L:, :] = sl
            mx_s[h, m, 0:1, :] = jnp.maximum(jnp.max(sc, axis=0, keepdims=True),
                                             jnp.max(sl, axis=0, keepdims=True))

    def values(h):
        e1 = jnp.exp2(s_s[h, 0] - mx_s[h, 0, 0:1, :]).astype(BF16)
        e2 = jnp.exp2(s_s[h, 1] - mx_s[h, 1, 0:1, :]).astype(BF16)
        r1 = jnp.dot(vt_s[h], e1, preferred_element_type=F32)
        r2 = jnp.dot(vt_s[h], e2, preferred_element_type=F32)
        ot_s[h * HEAD_DIM:(h + 1) * HEAD_DIM, :] = (
            r1[0:HEAD_DIM] * (1.0 / r1[HEAD_DIM:HEAD_DIM + 1])
            - r2[0:HEAD_DIM] * (lam / r2[HEAD_DIM:HEAD_DIM + 1]))

    def finish(i):
        o_ref[0, tile_rows(i), :] = _subln(ot_s[...], bd_ref, g_ref, lam_init)

    def tile(i, first):
        scores(i, 0)
        if not first:
            values(3)
            finish(i - 1)
        scores(i, 1)
        values(0)
        scores(i, 2)
        values(1)
        scores(i, 3)
        values(2)

    tile(0, True)

    def loop_body(i, carry):
        tile(i, False)
        return carry

    lax.fori_loop(1, n_tiles, loop_body, 0)
    values(3)
    finish(n_tiles - 1)


def _attention(z, zc, lam_params, subln_row, bd, *, lam_init, tq, ctx_kv_groups):
    B, S, _ = z.shape
    L = zc.shape[1]
    nk = L + S
    ck, cv = ctx_kv_groups
    return pl.pallas_call(
        functools.partial(_attn_body, lam_init=lam_init, tq=tq),
        grid=(B,),
        in_specs=[pl.BlockSpec((1, S, GROUP_W), lambda b: (b, 0, G_DQ)),
                  pl.BlockSpec((1, L, GROUP_W), lambda b: (b, 0, ck)),
                  pl.BlockSpec((1, L, GROUP_W), lambda b: (b, 0, cv)),
                  pl.BlockSpec((1, S, GROUP_W), lambda b: (b, 0, G_DK)),
                  pl.BlockSpec((1, S, GROUP_W), lambda b: (b, 0, G_DV)),
                  _const_spec((4, DIFF_QK)), _const_spec((1, GROUP_W)),
                  _const_spec((GROUP_W, GROUP_W))],
        out_specs=pl.BlockSpec((1, S, GROUP_W), lambda b: (b, 0, 0)),
        out_shape=jax.ShapeDtypeStruct((B, S, GROUP_W), BF16),
        scratch_shapes=[pltpu.VMEM((HEADS, 2, nk, tq), F32),
                        pltpu.VMEM((HEADS, V_ROWS, nk), BF16),
                        pltpu.VMEM((GROUP_W, tq), F32),
                        pltpu.VMEM((HEADS, 2, 8, tq), F32)],
        compiler_params=_cparams(("parallel",)),
        name="attn",
    )(z, zc, zc, z, z, lam_params, subln_row, bd)


def _tail_body(x_ref, xp_ref, xn_ref, ma_ref, map_ref, man_ref, md_ref, mdp_ref, mdn_ref,
               wo_ref, gt1_ref, g_ref, sh_ref, sc_ref, gt_ref, wu_ref, cw_ref, wd_ref,
               *rest, ts, final):
    if final:
        fg_ref, o_ref, x_s, m_s, h_s, a_s = rest
    else:
        o_ref, x_s, m_s, h_s, a_s = rest
    i = pl.program_id(1)
    last = pl.num_programs(1) - 1
    H = FFN_HALO
    k_abc = 3 * GROUP_W
    lo, mid, hi = slice(0, H), slice(H, H + ts), slice(H + ts, ts + 2 * H)
    for rows, xr, ar, dr in ((lo, xp_ref, map_ref, mdp_ref), (mid, x_ref, ma_ref, md_ref),
                             (hi, xn_ref, man_ref, mdn_ref)):
        x_s[rows, :] = xr[0]
        m_s[rows, 0:k_abc] = ar[0]
        m_s[rows, k_abc:] = dr[0]
    x1 = x_s[...] + gt1_ref[0] * jnp.dot(m_s[...], wo_ref[...], preferred_element_type=F32)
    x_s[...] = x1
    h = (_rms(x1) * g_ref[...]) * (1.0 + sc_ref[0]) + sh_ref[0]
    row = lax.broadcasted_iota(jnp.int32, (ts + 2 * H, 1), 0)
    inside = jnp.logical_and(jnp.logical_or(row >= H, i > 0),
                             jnp.logical_or(row < H + ts, i < last))
    h_s[...] = jnp.where(inside, h, 0.0).astype(BF16)

    a_s[...] = jnp.dot(h_s[...], wu_ref[:, 0:D_FF], preferred_element_type=F32)
    cw = cw_ref[...]
    conv = (a_s[pl.ds(H - 1, ts), :] * cw[0:1] + a_s[pl.ds(H, ts), :] * cw[1:2]
            + a_s[pl.ds(H + 1, ts), :] * cw[2:3])
    b = jnp.dot(h_s[mid, :], wu_ref[:, D_FF:], preferred_element_type=F32)
    hid = (jax.nn.silu(conv) * b).astype(BF16)
    y = x_s[mid, :] + gt_ref[0] * jnp.dot(hid, wd_ref[...], preferred_element_type=F32)
    if final:
        y = _rms(y) * fg_ref[...]
    o_ref[0] = y


def _tail(x, mabc, md, w_out_all, g2, mod, w_up_all, conv_w_all, w_down_all, final_g, l, *,
          ctx, ts):
    B, S, D = x.shape
    brow = (lambda b: CTX_ROW) if ctx else (lambda b: b)
    H = FFN_HALO
    nh = S // H
    per = ts // H
    final = final_g is not None

    def tiled(width):
        return [
            pl.BlockSpec((1, ts, width), lambda b, i: (b, i, 0)),
            pl.BlockSpec((1, H, width), lambda b, i: (b, jnp.maximum(i * per - 1, 0), 0)),
            pl.BlockSpec((1, H, width), lambda b, i: (b, jnp.minimum((i + 1) * per, nh - 1), 0)),
        ]

    def mod_spec(col):
        return pl.BlockSpec((1, 1, D), lambda b, i: (brow(b), 0, col))

    in_specs = tiled(D) + tiled(3 * GROUP_W) + tiled(GROUP_W) + [
        _layer_spec((D, D), l),
        mod_spec(2),
        _const_spec((1, D)),
        mod_spec(3),
        mod_spec(4),
        mod_spec(5),
        _layer_spec((D, 2 * D_FF), l),
        _layer_spec((3, D_FF), l),
        _layer_spec((D_FF, D), l),
    ]
    args = [x, x, x, mabc, mabc, mabc, md, md, md, w_out_all, mod, g2.reshape(1, D), mod, mod,
            mod, w_up_all, conv_w_all, w_down_all]
    if final:
        in_specs.append(_const_spec((1, D)))
        args.append(final_g.reshape(1, D))
    return pl.pallas_call(
        functools.partial(_tail_body, ts=ts, final=final),
        grid=(B, S // ts),
        in_specs=in_specs,
        out_specs=pl.BlockSpec((1, ts, D), lambda b, i: (b, i, 0)),
        out_shape=jax.ShapeDtypeStruct((B, S, D), F32),
        scratch_shapes=[pltpu.VMEM((ts + 2 * H, D), F32),
                        pltpu.VMEM((ts + 2 * H, D), BF16),
                        pltpu.VMEM((ts + 2 * H, D), BF16),
                        pltpu.VMEM((ts + 2 * H, D_FF), F32)],
        compiler_params=_cparams(("parallel", "parallel")),
        name="tail_ctx" if ctx else "tail",
    )(*args)


def _rope_tables(n):
    pos = jnp.arange(n, dtype=jnp.int32)
    row = (pos // GRID_W).astype(F32)
    col = (pos % GRID_W).astype(F32)
    inv_freq = ROPE_THETA ** (-jnp.arange(0, ROT_HALF, 2, dtype=F32) / ROT_HALF)
    lane = np.arange(GROUP_W)
    axis = (lane // ROT_HALF) % 2
    half = (lane // (ROT_HALF // 2)) % 2
    freq = jnp.tile(inv_freq, GROUP_W // (ROT_HALF // 2))[None, :]
    ang = jnp.where(jnp.asarray(axis == 0)[None, :], row[:, None] * freq, col[:, None] * freq)
    cos = jnp.cos(ang)
    sin = jnp.sin(ang)
    sa = jnp.where(jnp.asarray(half == 0)[None, :], -sin, 0.0)
    sb = jnp.where(jnp.asarray(half == 1)[None, :], sin, 0.0)
    return cos, sa, sb


def _dft_tables(n):
    n2c = GRID_W
    n1c = n // n2c
    k = jnp.arange(n, dtype=jnp.int32)

    def trig(cols, stride):
        kn = (k[:, None] * (jnp.arange(cols, dtype=jnp.int32) * stride)[None, :]) % n
        ang = kn.astype(F32) * (2.0 * math.pi / n)
        return jnp.cos(ang), jnp.sin(ang)

    ca, sa = trig(n1c, n2c)
    cb, sb = trig(n2c, 1)
    s = 1.0 / math.sqrt(n)
    cos = ca[:, :, None] * cb[:, None, :] - sa[:, :, None] * sb[:, None, :]
    sin = sa[:, :, None] * cb[:, None, :] + ca[:, :, None] * sb[:, None, :]
    return ((cos * s).astype(BF16).reshape(n, n), (sin * (-s)).astype(BF16).reshape(n, n))


def _head_block_tables():
    eye = np.kron(np.eye(HEADS), np.ones((HEAD_DIM, HEAD_DIM)))
    k = np.arange(HEAD_DIM)
    ang = 2.0 * np.pi * ((k[:, None] * k[None, :]) % HEAD_DIM) / HEAD_DIM
    s = 1.0 / math.sqrt(HEAD_DIM)
    c64 = np.kron(np.eye(HEADS), np.cos(ang) * s)
    s64 = np.kron(np.eye(HEADS), np.sin(ang) * s)
    return tuple(jnp.asarray(t, F32).astype(BF16) for t in (eye / HEAD_DIM, c64, s64))


def kernel(x, c, ctx, c_ctx, w_ada, b_ada, norm1_g, norm2_g, w_in, gmlp_ws, gmlp_bs, sconv_w,
           lambda_q1, lambda_k1, lambda_q2, lambda_k2, subln_g, w_out, ffn_w_up, ffn_conv_w,
           ffn_w_down, final_g):
    B, N, D = x.shape
    L = ctx.shape[1]

    cc = jnp.zeros((ADA_ROWS, D), F32).at[:B].set(c).at[CTX_ROW].set(c_ctx)
    mod_all = _adaln(cc, w_ada, b_ada)

    rope_tabs = _rope_tables(N)
    bd, c64, s64 = _head_block_tables()
    dft_n = _dft_tables(N)
    dft_l = _dft_tables(L)

    w_in_all = w_in.astype(BF16)
    w_out_all = w_out.astype(BF16)
    w_up_all = ffn_w_up.astype(BF16)
    w_down_all = ffn_w_down.astype(BF16)

    xc = ctx
    for l in range(DEPTH):
        ctx_out = l < DEPTH - 1
        lam_init = 0.8 - 0.6 * math.exp(-0.3 * l)
        mod = mod_all[l].reshape(ADA_ROWS, 1, 6 * D)
        ws_rows = gmlp_ws[l].reshape(HEADS * CHUNK, CHUNK).astype(BF16)
        bias_full = jnp.repeat(gmlp_bs[l].T, HEAD_DIM, axis=1)
        lam_params = jnp.stack([lambda_q1[l], lambda_k1[l], lambda_q2[l], lambda_k2[l]])
        subln_row = jnp.tile(subln_g[l], HEADS).reshape(1, GROUP_W)
        fin = final_g if l == DEPTH - 1 else None

        z = _in_proj(x, norm1_g[l], mod, w_in_all, l, rope_tabs, ctx=False, ts=IN_ROWS)
        xc_rows = xc.reshape(1, B * L, D)
        if ctx_out:
            zc = _in_proj(xc_rows, norm1_g[l], mod, w_in_all, l, None, ctx=True, ts=CTX_IN_ROWS)
            ctx_kv_groups = (G_DK, G_DV)
        else:
            zc = _in_proj_ctx_kv(xc_rows, norm1_g[l], mod, w_in_all, l, ts=CTX_IN_ROWS)
            ctx_kv_groups = (0, 1)
        zc = zc.reshape(B, L, zc.shape[-1])

        mabc = _mixers(z, ws_rows, bias_full, sconv_w[l], bd, c64, s64, *dft_n, ctx=False)
        md = _attention(z, zc, lam_params, subln_row, bd, lam_init=lam_init, tq=ATTN_QUERIES,
                        ctx_kv_groups=ctx_kv_groups)
        x = _tail(x, mabc, md, w_out_all, norm2_g[l], mod, w_up_all, ffn_conv_w, w_down_all,
                  fin, l, ctx=False, ts=TAIL_ROWS)

        if ctx_out:
            mabc_c = _mixers(zc, ws_rows, bias_full, sconv_w[l], bd, c64, s64, *dft_l, ctx=True)
            md_c = _attention_ctx(zc, lam_params, subln_row, bd, lam_init=lam_init)
            xc = _tail(xc, mabc_c, md_c, w_out_all, norm2_g[l], mod, w_up_all, ffn_conv_w,
                       w_down_all, None, l, ctx=True, ts=L)
    return x
```

```python
import functools
import math

import numpy as np
import jax
import jax.numpy as jnp
from jax import lax
from jax.experimental import pallas as pl
from jax.experimental.pallas import tpu as pltpu

D_MODEL = 1024
DEPTH = 2
GRID_W = 64
N_GROUPS = 4
GROUP_W = D_MODEL // N_GROUPS
HEADS = 4
HEAD_DIM = GROUP_W // HEADS
CHUNK = 128
DIFF_QK = HEAD_DIM // 2
ROT_HALF = DIFF_QK // 2
ROPE_THETA = 10000.0
D_FF = 2816
EPS = 1e-6
G_AU, G_AV, G_BX, G_BB, G_BC, G_CF, G_DQ, G_DK, G_DV = range(9)
IN_COLS = 9 * GROUP_W

F32 = jnp.float32
BF16 = jnp.bfloat16

ADA_ROWS = 24
CTX_ROW = 16
BF16_ROWS = 16
FFN_HALO = BF16_ROWS
ONES_ROWS = BF16_ROWS
V_ROWS = HEAD_DIM + ONES_ROWS
VMEM_LIMIT = 56 * 1024 * 1024
QK_SCALE = DIFF_QK ** -0.5 * math.log2(math.e)
NT_DIMS = (((1,), (1,)), ((), ()))

IN_ROWS = 2048
IN_SUB = 128
CTX_IN_ROWS = 1024
DFT_ROWS = 256
ATTN_QUERIES = 256
TAIL_ROWS = 512


def _cparams(sem):
    return pltpu.CompilerParams(dimension_semantics=sem, vmem_limit_bytes=VMEM_LIMIT)


def _const_spec(shape):
    nd = len(shape)
    return pl.BlockSpec(shape, lambda *_: (0,) * nd, pipeline_mode=pl.Buffered(1))


def _layer_spec(shape, l):
    nd = len(shape)
    return pl.BlockSpec((None,) + tuple(shape), lambda *_: (l,) + (0,) * nd,
                        pipeline_mode=pl.Buffered(1))


def _rms(x):
    return x * lax.rsqrt(jnp.mean(x * x, axis=-1, keepdims=True) + EPS)


def _group_mean(sq, bd):
    hi = sq.astype(BF16)
    lo = (sq - hi.astype(F32)).astype(BF16)
    return (jnp.dot(hi, bd, preferred_element_type=F32)
            + jnp.dot(lo, bd, preferred_element_type=F32))


def _ada_body(c_ref, w_ref, b_ref, o_ref):
    s = jax.nn.silu(c_ref[...]).astype(BF16)
    o_ref[0] = jnp.dot(s, w_ref[0].astype(BF16), preferred_element_type=F32) + b_ref[0]


def _adaln(cc, w_ada, b_ada):
    n_col = 6
    return pl.pallas_call(
        _ada_body,
        grid=(DEPTH, n_col),
        in_specs=[
            pl.BlockSpec((ADA_ROWS, D_MODEL), lambda l, j: (0, 0)),
            pl.BlockSpec((1, D_MODEL, D_MODEL), lambda l, j: (l, 0, j)),
            pl.BlockSpec((1, 1, D_MODEL), lambda l, j: (l, 0, j)),
        ],
        out_specs=pl.BlockSpec((1, ADA_ROWS, D_MODEL), lambda l, j: (l, 0, j)),
        out_shape=jax.ShapeDtypeStruct((DEPTH, ADA_ROWS, 6 * D_MODEL), F32),
        compiler_params=_cparams(("parallel", "parallel")),
        name="adaln",
    )(cc, w_ada, b_ada.reshape(DEPTH, 1, 6 * D_MODEL))


def _in_body(x_ref, g_ref, sh_ref, sc_ref, w_ref, *rest, rope):
    if rope:
        cos_ref, sa_ref, sb_ref, z_ref = rest
    else:
        (z_ref,) = rest
    ts = x_ref.shape[1]
    sub = min(ts, IN_SUB)
    q0 = G_DQ * GROUP_W
    for r in range(0, ts, sub):
        rows = slice(r, r + sub)
        h = _rms(x_ref[0, rows, :]) * g_ref[...]
        h = h * (1.0 + sc_ref[0]) + sh_ref[0]
        z = jnp.dot(h.astype(BF16), w_ref[...], preferred_element_type=F32)
        z_ref[0, rows, :q0] = z[:, :q0].astype(BF16)
        for g in (G_DQ, G_DK):
            t = z[:, g * GROUP_W:(g + 1) * GROUP_W]
            if rope:
                t = (t * cos_ref[rows, :]
                     + pltpu.roll(t, GROUP_W - ROT_HALF // 2, axis=1) * sa_ref[rows, :]
                     + pltpu.roll(t, ROT_HALF // 2, axis=1) * sb_ref[rows, :])
            if g == G_DQ:
                t = t * QK_SCALE
            z_ref[0, rows, g * GROUP_W:(g + 1) * GROUP_W] = t.astype(BF16)
        z_ref[0, rows, G_DV * GROUP_W:] = z[:, G_DV * GROUP_W:].astype(BF16)


def _in_kv_body(x_ref, g_ref, sh_ref, sc_ref, wk_ref, wv_ref, z_ref):
    h = _rms(x_ref[0]) * g_ref[...]
    h = (h * (1.0 + sc_ref[0]) + sh_ref[0]).astype(BF16)
    z_ref[0, :, 0:GROUP_W] = jnp.dot(h, wk_ref[...], preferred_element_type=F32).astype(BF16)
    z_ref[0, :, GROUP_W:] = jnp.dot(h, wv_ref[...], preferred_element_type=F32).astype(BF16)


def _in_proj_ctx_kv(xc, g1, mod, w_in_all, l, *, ts):
    B, S, D = xc.shape

    def w_spec(grp):
        return pl.BlockSpec((None, D, GROUP_W), lambda b, i: (l, 0, grp),
                            pipeline_mode=pl.Buffered(1))

    return pl.pallas_call(
        _in_kv_body,
        grid=(B, S // ts),
        in_specs=[
            pl.BlockSpec((1, ts, D), lambda b, i: (b, i, 0)),
            _const_spec((1, D)),
            pl.BlockSpec((1, 1, D), lambda b, i: (CTX_ROW, 0, 0)),
            pl.BlockSpec((1, 1, D), lambda b, i: (CTX_ROW, 0, 1)),
            w_spec(G_DK),
            w_spec(G_DV),
        ],
        out_specs=pl.BlockSpec((1, ts, 2 * GROUP_W), lambda b, i: (b, i, 0)),
        out_shape=jax.ShapeDtypeStruct((B, S, 2 * GROUP_W), BF16),
        compiler_params=_cparams(("parallel", "parallel")),
        name="in_proj_ctx_kv",
    )(xc, g1.reshape(1, D), mod, mod, w_in_all, w_in_all)


def _in_proj(x, g1, mod, w_in_all, l, rope_tabs, *, ctx, ts):
    B, S, D = x.shape
    brow = (lambda b: CTX_ROW) if ctx else (lambda b: b)
    in_specs = [
        pl.BlockSpec((1, ts, D), lambda i, b: (b, i, 0)),
        _const_spec((1, D)),
        pl.BlockSpec((1, 1, D), lambda i, b: (brow(b), 0, 0)),
        pl.BlockSpec((1, 1, D), lambda i, b: (brow(b), 0, 1)),
        _layer_spec((D, IN_COLS), l),
    ]
    args = [x, g1.reshape(1, D), mod, mod, w_in_all]
    rope = rope_tabs is not None
    if rope:
        in_specs += [pl.BlockSpec((ts, GROUP_W), lambda i, b: (i, 0))] * 3
        args += list(rope_tabs)
    return pl.pallas_call(
        functools.partial(_in_body, rope=rope),
        grid=(S // ts, B),
        in_specs=in_specs,
        out_specs=pl.BlockSpec((1, ts, IN_COLS), lambda i, b: (b, i, 0)),
        out_shape=jax.ShapeDtypeStruct((B, S, IN_COLS), BF16),
        compiler_params=_cparams(("parallel", "parallel")),
        name="in_proj_ctx" if ctx else "in_proj",
    )(*args)


def _mix_body(z_ref, ws_ref, bias_ref, sc_ref, bd_ref, c64_ref, s64_ref, wc_ref, wsn_ref,
              o_ref, v_s, gc_s, gs_s, *, seq):
    lane = lax.broadcasted_iota(jnp.int32, (1, GROUP_W), 1)
    head_of_lane = lax.shift_right_logical(lane, 6)
    ws = ws_ref[...]
    bias = bias_ref[...]

    tile = min(seq, DFT_ROWS)
    n_t = seq // tile
    per = tile // CHUNK

    def zcols(rows, grp):
        return z_ref[0, rows, grp * GROUP_W:(grp + 1) * GROUP_W]

    def head_norm(rows):
        g = jax.nn.gelu(zcols(rows, G_AV).astype(F32))
        v_s[rows, :] = (g * lax.rsqrt(_group_mean(g * g, bd_ref[...]) + EPS)).astype(BF16)

    def gated_conv(t):
        rows = slice(t * tile, (t + 1) * tile)

        def cx_of(rs):
            return zcols(rs, G_BC).astype(F32) * zcols(rs, G_BX).astype(F32)

        cx = cx_of(rows)
        zero = jnp.zeros((1, GROUP_W), F32)
        nb = BF16_ROWS
        before = cx_of(slice(t * tile - nb, t * tile))[nb - 1:nb] if t > 0 else zero
        after = cx_of(slice((t + 1) * tile, (t + 1) * tile + nb))[0:1] if t < n_t - 1 else zero
        row = lax.broadcasted_iota(jnp.int32, (tile, 1), 0)
        prev = jnp.where(row == 0, before, pltpu.roll(cx, 1, axis=0))
        nxt = jnp.where(row == tile - 1, after, pltpu.roll(cx, tile - 1, axis=0))
        w = sc_ref[...]
        conv = prev * w[0:1] + cx * w[1:2] + nxt * w[2:3]
        o_ref[0, rows, GROUP_W:2 * GROUP_W] = (zcols(rows, G_BB).astype(F32) * conv).astype(BF16)

    def gate_chunk(c):
        r = c * CHUNK
        m4 = jnp.dot(ws, v_s[r:r + CHUNK, :], preferred_element_type=F32)
        mixed = m4[0:CHUNK]
        for h in range(1, HEADS):
            mixed = jnp.where(head_of_lane == h, m4[h * CHUNK:(h + 1) * CHUNK], mixed)
        u = jax.nn.gelu(z_ref[0, r:r + CHUNK, G_AU * GROUP_W:(G_AU + 1) * GROUP_W].astype(F32))
        o_ref[0, r:r + CHUNK, 0:GROUP_W] = (u * (mixed + bias)).astype(BF16)

    f = zcols(slice(None), G_CF)
    gc_s[...] = jnp.dot(f, c64_ref[...], preferred_element_type=F32).astype(BF16)
    gs_s[...] = jnp.dot(f, s64_ref[...], preferred_element_type=F32).astype(BF16)
    for t in range(n_t):
        rows = slice(t * tile, (t + 1) * tile)
        fo = (jnp.dot(wc_ref[rows, :], gc_s[...], preferred_element_type=F32)
              + jnp.dot(wsn_ref[rows, :], gs_s[...], preferred_element_type=F32))
        o_ref[0, rows, 2 * GROUP_W:3 * GROUP_W] = fo.astype(BF16)
        head_norm(rows)
        for c in range(t * per, (t + 1) * per):
            gate_chunk(c)
        gated_conv(t)


def _mixers(z, ws_rows, bias_full, sconv, bd, c64, s64, wc, wsn, *, ctx):
    B, S, _ = z.shape
    zc = (G_CF + 1) * GROUP_W
    return pl.pallas_call(
        functools.partial(_mix_body, seq=S),
        grid=(B,),
        in_specs=[
            pl.BlockSpec((1, S, zc), lambda b: (b, 0, 0)),
            _const_spec((HEADS * CHUNK, CHUNK)),
            _const_spec((CHUNK, GROUP_W)),
            _const_spec((3, GROUP_W)),
            _const_spec((GROUP_W, GROUP_W)),
            _const_spec((GROUP_W, GROUP_W)),
            _const_spec((GROUP_W, GROUP_W)),
            _const_spec((S, S)),
            _const_spec((S, S)),
        ],
        out_specs=pl.BlockSpec((1, S, 3 * GROUP_W), lambda b: (b, 0, 0)),
        out_shape=jax.ShapeDtypeStruct((B, S, 3 * GROUP_W), BF16),
        scratch_shapes=[pltpu.VMEM((S, GROUP_W), BF16)] * 3,
        compiler_params=_cparams(("parallel",)),
        name="mixers_ctx" if ctx else "mixers",
    )(z, ws_rows, bias_full, sconv, bd, c64, s64, wc, wsn)


def _lambda(lam_ref, lam_init):
    lp = lam_ref[...]
    return (jnp.exp(jnp.sum(lp[0:1] * lp[1:2], axis=-1, keepdims=True))
            - jnp.exp(jnp.sum(lp[2:3] * lp[3:4], axis=-1, keepdims=True)) + lam_init)


def _softmax_diff(s_maps, lam):
    es, cs = [], []
    for st in s_maps:
        e = jnp.exp2(st - jnp.max(st, axis=0, keepdims=True))
        es.append(e)
        cs.append(jnp.sum(e, axis=0, keepdims=True))
    return (es[0] * (1.0 / cs[0]) - es[1] * (lam / cs[1])).astype(BF16)


def _subln(ot, bd_ref, g_ref, lam_init):
    o = ot.T
    y = o * lax.rsqrt(_group_mean(o * o, bd_ref[...]) + EPS) * g_ref[...]
    return (y * (1.0 - lam_init)).astype(BF16)


def _attn_ctx_body(q_ref, k_ref, v_ref, lam_ref, g_ref, bd_ref, o_ref, *, lam_init):
    q = q_ref[0]
    kall = k_ref[0]
    vt = v_ref[0].astype(F32).T.astype(BF16)
    lam = _lambda(lam_ref, lam_init)
    lane = lax.broadcasted_iota(jnp.int32, (1, GROUP_W), 1)
    map_of_lane = lax.shift_right_logical(lane, 5)
    outs = []
    for h in range(HEADS):
        s_maps = [lax.dot_general(kall, jnp.where(map_of_lane == 2 * h + m, q, jnp.zeros_like(q)),
                                  NT_DIMS, preferred_element_type=F32) for m in range(2)]
        at = _softmax_diff(s_maps, lam)
        outs.append(jnp.dot(vt[h * HEAD_DIM:(h + 1) * HEAD_DIM, :], at,
                            preferred_element_type=F32))
    o_ref[0] = _subln(jnp.concatenate(outs, axis=0), bd_ref, g_ref, lam_init)


def _attention_ctx(zc, lam_params, subln_row, bd, *, lam_init):
    B, L, _ = zc.shape
    return pl.pallas_call(
        functools.partial(_attn_ctx_body, lam_init=lam_init),
        grid=(B,),
        in_specs=[pl.BlockSpec((1, L, GROUP_W), lambda b: (b, 0, G_DQ)),
                  pl.BlockSpec((1, L, GROUP_W), lambda b: (b, 0, G_DK)),
                  pl.BlockSpec((1, L, GROUP_W), lambda b: (b, 0, G_DV)),
                  _const_spec((4, DIFF_QK)), _const_spec((1, GROUP_W)),
                  _const_spec((GROUP_W, GROUP_W))],
        out_specs=pl.BlockSpec((1, L, GROUP_W), lambda b: (b, 0, 0)),
        out_shape=jax.ShapeDtypeStruct((B, L, GROUP_W), BF16),
        compiler_params=_cparams(("parallel",)),
        name="attn_ctx",
    )(zc, zc, zc, lam_params, subln_row, bd)


def _attn_body(q_ref, kc_ref, vc_ref, k_ref, v_ref, lam_ref, g_ref, bd_ref, o_ref,
               s_s, vt_s, ot_s, mx_s, *, lam_init, tq):
    L = kc_ref.shape[1]
    S = k_ref.shape[1]
    n_tiles = S // tq
    vt = jnp.concatenate([vc_ref[0].astype(F32).T, v_ref[0].astype(F32).T], axis=1)
    for h in range(HEADS):
        vt_s[h, 0:HEAD_DIM, :] = vt[h * HEAD_DIM:(h + 1) * HEAD_DIM, :].astype(BF16)
        vt_s[h, HEAD_DIM:, :] = jnp.ones((ONES_ROWS, L + S), BF16)
    lam = _lambda(lam_ref, lam_init)
    lane = lax.broadcasted_iota(jnp.int32, (1, GROUP_W), 1)
    map_of_lane = lax.shift_right_logical(lane, 5)

    def tile_rows(i):
        r = i * tq
        return pl.ds(r if isinstance(r, int) else pl.multiple_of(r, tq), tq)

    def scores_map(i, h, m):
        q = q_ref[0, tile_rows(i), :]
        qm = jnp.where(map_of_lane == 2 * h + m, q, jnp.zeros_like(q))
        sc = lax.dot_general(kc_ref[0], qm, NT_DIMS, preferred_element_type=F32)
        sl = lax.dot_general(k_ref[0], qm, NT_DIMS, preferred_element_type=F32)
        s_s[h, m, 0:L, :] = sc
        s_s[h, m, L:, :] = sl
        mx_s[h, m, 0:1, :] = jnp.maximum(jnp.max(sc, axis=0, keepdims=True),
                                         jnp.max(sl, axis=0, keepdims=True))

    def values_map(h, m):
        e = jnp.exp2(s_s[h, m] - mx_s[h, m, 0:1, :]).astype(BF16)
        return jnp.dot(vt_s[h], e, preferred_element_type=F32)

    def combine(h, r1, r2):
        ot_s[h * HEAD_DIM:(h + 1) * HEAD_DIM, :] = (
            r1[0:HEAD_DIM] * (1.0 / r1[HEAD_DIM:HEAD_DIM + 1])
            - r2[0:HEAD_DIM] * (lam / r2[HEAD_DIM:HEAD_DIM + 1]))

    def scores(i, h):
        scores_map(i, h, 0)
        scores_map(i, h, 1)

    def values(h):
        combine(h, values_map(h, 0), values_map(h, 1))

    def step(i, h, hp):
        scores_map(i, h, 0)
        r1 = values_map(hp, 0)
        scores_map(i, h, 1)
        r2 = values_map(hp, 1)
        combine(hp, r1, r2)

    def finish(i):
        o_ref[0, tile_rows(i), :] = _subln(ot_s[...], bd_ref, g_ref, lam_init)

    def tile(i, first):
        if first:
            scores(i, 0)
        else:
            step(i, 0, 3)
            finish(i - 1)
        step(i, 1, 0)
        step(i, 2, 1)
        step(i, 3, 2)

    tile(0, True)

    def loop_body(i, carry):
        tile(i, False)
        return carry

    lax.fori_loop(1, n_tiles, loop_body, 0)
    values(3)
    finish(n_tiles - 1)


def _attention(z, zc, lam_params, subln_row, bd, *, lam_init, tq, ctx_kv_groups):
    B, S, _ = z.shape
    L = zc.shape[1]
    nk = L + S
    ck, cv = ctx_kv_groups
    return pl.pallas_call(
        functools.partial(_attn_body, lam_init=lam_init, tq=tq),
        grid=(B,),
        in_specs=[pl.BlockSpec((1, S, GROUP_W), lambda b: (b, 0, G_DQ)),
                  pl.BlockSpec((1, L, GROUP_W), lambda b: (b, 0, ck)),
                  pl.BlockSpec((1, L, GROUP_W), lambda b: (b, 0, cv)),
                  pl.BlockSpec((1, S, GROUP_W), lambda b: (b, 0, G_DK)),
                  pl.BlockSpec((1, S, GROUP_W), lambda b: (b, 0, G_DV)),
                  _const_spec((4, DIFF_QK)), _const_spec((1, GROUP_W)),
                  _const_spec((GROUP_W, GROUP_W))],
        out_specs=pl.BlockSpec((1, S, GROUP_W), lambda b: (b, 0, 0)),
        out_shape=jax.ShapeDtypeStruct((B, S, GROUP_W), BF16),
        scratch_shapes=[pltpu.VMEM((HEADS, 2, nk, tq), F32),
                        pltpu.VMEM((HEADS, V_ROWS, nk), BF16),
                        pltpu.VMEM((GROUP_W, tq), F32),
                        pltpu.VMEM((HEADS, 2, 8, tq), F32)],
        compiler_params=_cparams(("parallel",)),
        name="attn",
    )(z, zc, zc, z, z, lam_params, subln_row, bd)


def _tail_body(x_ref, xp_ref, xn_ref, ma_ref, map_ref, man_ref, md_ref, mdp_ref, mdn_ref,
               wo_ref, gt1_ref, g_ref, sh_ref, sc_ref, gt_ref, wu_ref, cw_ref, wd_ref,
               *rest, ts, final):
    if final:
        fg_ref, o_ref, x_s, m_s, h_s, a_s = rest
    else:
        o_ref, x_s, m_s, h_s, a_s = rest
    i = pl.program_id(1)
    last = pl.num_programs(1) - 1
    H = FFN_HALO
    k_abc = 3 * GROUP_W
    lo, mid, hi = slice(0, H), slice(H, H + ts), slice(H + ts, ts + 2 * H)
    for rows, xr, ar, dr in ((lo, xp_ref, map_ref, mdp_ref), (mid, x_ref, ma_ref, md_ref),
                             (hi, xn_ref, man_ref, mdn_ref)):
        x_s[rows, :] = xr[0]
        m_s[rows, 0:k_abc] = ar[0]
        m_s[rows, k_abc:] = dr[0]
    x1 = x_s[...] + gt1_ref[0] * jnp.dot(m_s[...], wo_ref[...], preferred_element_type=F32)
    x_s[...] = x1
    h = (_rms(x1) * g_ref[...]) * (1.0 + sc_ref[0]) + sh_ref[0]
    row = lax.broadcasted_iota(jnp.int32, (ts + 2 * H, 1), 0)
    inside = jnp.logical_and(jnp.logical_or(row >= H, i > 0),
                             jnp.logical_or(row < H + ts, i < last))
    h_s[...] = jnp.where(inside, h, 0.0).astype(BF16)

    a_s[...] = jnp.dot(h_s[...], wu_ref[:, 0:D_FF], preferred_element_type=F32)
    cw = cw_ref[...]
    conv = (a_s[pl.ds(H - 1, ts), :] * cw[0:1] + a_s[pl.ds(H, ts), :] * cw[1:2]
            + a_s[pl.ds(H + 1, ts), :] * cw[2:3])
    b = jnp.dot(h_s[mid, :], wu_ref[:, D_FF:], preferred_element_type=F32)
    hid = (jax.nn.silu(conv) * b).astype(BF16)
    y = x_s[mid, :] + gt_ref[0] * jnp.dot(hid, wd_ref[...], preferred_element_type=F32)
    if final:
        y = _rms(y) * fg_ref[...]
    o_ref[0] = y


def _tail(x, mabc, md, w_out_all, g2, mod, w_up_all, conv_w_all, w_down_all, final_g, l, *,
          ctx, ts):
    B, S, D = x.shape
    brow = (lambda b: CTX_ROW) if ctx else (lambda b: b)
    H = FFN_HALO
    nh = S // H
    per = ts // H
    final = final_g is not None

    def tiled(width):
        return [
            pl.BlockSpec((1, ts, width), lambda b, i: (b, i, 0)),
            pl.BlockSpec((1, H, width), lambda b, i: (b, jnp.maximum(i * per - 1, 0), 0)),
            pl.BlockSpec((1, H, width), lambda b, i: (b, jnp.minimum((i + 1) * per, nh - 1), 0)),
        ]

    def mod_spec(col):
        return pl.BlockSpec((1, 1, D), lambda b, i: (brow(b), 0, col))

    in_specs = tiled(D) + tiled(3 * GROUP_W) + tiled(GROUP_W) + [
        _layer_spec((D, D), l),
        mod_spec(2),
        _const_spec((1, D)),
        mod_spec(3),
        mod_spec(4),
        mod_spec(5),
        _layer_spec((D, 2 * D_FF), l),
        _layer_spec((3, D_FF), l),
        _layer_spec((D_FF, D), l),
    ]
    args = [x, x, x, mabc, mabc, mabc, md, md, md, w_out_all, mod, g2.reshape(1, D), mod, mod,
            mod, w_up_all, conv_w_all, w_down_all]
    if final:
        in_specs.append(_const_spec((1, D)))
        args.append(final_g.reshape(1, D))
    return pl.pallas_call(
        functools.partial(_tail_body, ts=ts, final=final),
        grid=(B, S // ts),
        in_specs=in_specs,
        out_specs=pl.BlockSpec((1, ts, D), lambda b, i: (b, i, 0)),
        out_shape=jax.ShapeDtypeStruct((B, S, D), F32),
        scratch_shapes=[pltpu.VMEM((ts + 2 * H, D), F32),
                        pltpu.VMEM((ts + 2 * H, D), BF16),
                        pltpu.VMEM((ts + 2 * H, D), BF16),
                        pltpu.VMEM((ts + 2 * H, D_FF), F32)],
        compiler_params=_cparams(("parallel", "parallel")),
        name="tail_ctx" if ctx else "tail",
    )(*args)


def _rope_tables(n):
    pos = jnp.arange(n, dtype=jnp.int32)
    row = (pos // GRID_W).astype(F32)
    col = (pos % GRID_W).astype(F32)
    inv_freq = ROPE_THETA ** (-jnp.arange(0, ROT_HALF, 2, dtype=F32) / ROT_HALF)
    lane = np.arange(GROUP_W)
    axis = (lane // ROT_HALF) % 2
    half = (lane // (ROT_HALF // 2)) % 2
    freq = jnp.tile(inv_freq, GROUP_W // (ROT_HALF // 2))[None, :]
    ang = jnp.where(jnp.asarray(axis == 0)[None, :], row[:, None] * freq, col[:, None] * freq)
    cos = jnp.cos(ang)
    sin = jnp.sin(ang)
    sa = jnp.where(jnp.asarray(half == 0)[None, :], -sin, 0.0)
    sb = jnp.where(jnp.asarray(half == 1)[None, :], sin, 0.0)
    return cos, sa, sb


def _dft_tables(n):
    n2c = GRID_W
    n1c = n // n2c
    k = jnp.arange(n, dtype=jnp.int32)

    def trig(cols, stride):
        kn = (k[:, None] * (jnp.arange(cols, dtype=jnp.int32) * stride)[None, :]) % n
        ang = kn.astype(F32) * (2.0 * math.pi / n)
        return jnp.cos(ang), jnp.sin(ang)

    ca, sa = trig(n1c, n2c)
    cb, sb = trig(n2c, 1)
    s = 1.0 / math.sqrt(n)
    cos = ca[:, :, None] * cb[:, None, :] - sa[:, :, None] * sb[:, None, :]
    sin = sa[:, :, None] * cb[:, None, :] + ca[:, :, None] * sb[:, None, :]
    return ((cos * s).astype(BF16).reshape(n, n), (sin * (-s)).astype(BF16).reshape(n, n))


def _head_block_tables():
    eye = np.kron(np.eye(HEADS), np.ones((HEAD_DIM, HEAD_DIM)))
    k = np.arange(HEAD_DIM)
    ang = 2.0 * np.pi * ((k[:, None] * k[None, :]) % HEAD_DIM) / HEAD_DIM
    s = 1.0 / math.sqrt(HEAD_DIM)
    c64 = np.kron(np.eye(HEADS), np.cos(ang) * s)
    s64 = np.kron(np.eye(HEADS), np.sin(ang) * s)
    return tuple(jnp.asarray(t, F32).astype(BF16) for t in (eye / HEAD_DIM, c64, s64))


def kernel(x, c, ctx, c_ctx, w_ada, b_ada, norm1_g, norm2_g, w_in, gmlp_ws, gmlp_bs, sconv_w,
           lambda_q1, lambda_k1, lambda_q2, lambda_k2, subln_g, w_out, ffn_w_up, ffn_conv_w,
           ffn_w_down, final_g):
    B, N, D = x.shape
    L = ctx.shape[1]

    cc = jnp.zeros((ADA_ROWS, D), F32).at[:B].set(c).at[CTX_ROW].set(c_ctx)
    mod_all = _adaln(cc, w_ada, b_ada)

    rope_tabs = _rope_tables(N)
    bd, c64, s64 = _head_block_tables()
    dft_n = _dft_tables(N)
    dft_l = _dft_tables(L)

    w_in_all = w_in.astype(BF16)
    w_out_all = w_out.astype(BF16)
    w_up_all = ffn_w_up.astype(BF16)
    w_down_all = ffn_w_down.astype(BF16)

    xc = ctx
    for l in range(DEPTH):
        ctx_out = l < DEPTH - 1
        lam_init = 0.8 - 0.6 * math.exp(-0.3 * l)
        mod = mod_all[l].reshape(ADA_ROWS, 1, 6 * D)
        ws_rows = gmlp_ws[l].reshape(HEADS * CHUNK, CHUNK).astype(BF16)
        bias_full = jnp.repeat(gmlp_bs[l].T, HEAD_DIM, axis=1)
        lam_params = jnp.stack([lambda_q1[l], lambda_k1[l], lambda_q2[l], lambda_k2[l]])
        subln_row = jnp.tile(subln_g[l], HEADS).reshape(1, GROUP_W)
        fin = final_g if l == DEPTH - 1 else None

        z = _in_proj(x, norm1_g[l], mod, w_in_all, l, rope_tabs, ctx=False, ts=IN_ROWS)
        xc_rows = xc.reshape(1, B * L, D)
        if ctx_out:
            zc = _in_proj(xc_rows, norm1_g[l], mod, w_in_all, l, None, ctx=True, ts=CTX_IN_ROWS)
            ctx_kv_groups = (G_DK, G_DV)
        else:
            zc = _in_proj_ctx_kv(xc_rows, norm1_g[l], mod, w_in_all, l, ts=CTX_IN_ROWS)
            ctx_kv_groups = (0, 1)
        zc = zc.reshape(B, L, zc.shape[-1])

        mabc = _mixers(z, ws_rows, bias_full, sconv_w[l], bd, c64, s64, *dft_n, ctx=False)
        md = _attention(z, zc, lam_params, subln_row, bd, lam_init=lam_init, tq=ATTN_QUERIES,
                        ctx_kv_groups=ctx_kv_groups)
        x = _tail(x, mabc, md, w_out_all, norm2_g[l], mod, w_up_all, ffn_conv_w, w_down_all,
                  fin, l, ctx=False, ts=TAIL_ROWS)

        if ctx_out:
            mabc_c = _mixers(zc, ws_rows, bias_full, sconv_w[l], bd, c64, s64, *dft_l, ctx=True)
            md_c = _attention_ctx(zc, lam_params, subln_row, bd, lam_init=lam_init)
            xc = _tail(xc, mabc_c, md_c, w_out_all, norm2_g[l], mod, w_up_all, ffn_conv_w,
                       w_down_all, None, l, ctx=True, ts=L)
    return x
```
